```python
import math
import jax, jax.numpy as jnp
from jax import lax
import numpy as np

D_MODEL = 4096
BATCH = 4
SEQ = 2048
DEPTH = 1
DEC_BATCH = 16
DEC_SEQ = 64
PAST_LEN = 1024

CHUNK = 64
D_CONV = D_MODEL // 2
CONV_W = 3
D_POOL = D_MODEL // 2
POOL_WINDOWS = (2, 4, 8, 16)
POOL_GROUPS = len(POOL_WINDOWS)
POOL_GROUP_DIM = D_POOL // POOL_GROUPS
POOL_MAX = max(POOL_WINDOWS)
POOL_OUT_GROUP = D_MODEL // POOL_GROUPS
IN_SPLITS = (D_CONV, 2 * D_CONV, 3 * D_CONV, 3 * D_CONV + D_POOL, 3 * D_CONV + D_POOL + D_MODEL)
IN_COLS = 3 * D_CONV + D_POOL + 2 * D_MODEL
N_EXPERTS = 32
TOP_K = 4
D_FF = D_MODEL
SWIGLU_ALPHA = 1.702
SWIGLU_LIMIT = 7.0
EXPERT_BLOCK = 256
EPS = 1e-5

kernel_name = 'streaming_conv_pool_moe_step'


def rms_norm(x, g):
    xf = x.astype(jnp.float32)
    y = xf * lax.rsqrt(jnp.mean(xf * xf, axis=-1, keepdims=True) + EPS)
    return (y * g.astype(jnp.float32)).astype(x.dtype)


def token_mixer(xn, conv_hist, pool_hist, pos0, w_in, conv_k, w_conv_out, w_pool_map, pool_scale, w_o):
    b, L, _ = xn.shape
    proj = jnp.einsum('bld,de->ble', xn, w_in)
    gate_b, gate_c, x_c, u, g_a, g_b = jnp.split(proj, IN_SPLITS, axis=-1)

    v = gate_c * x_c
    v_ext = jnp.concatenate([conv_hist.astype(v.dtype), v], axis=1)
    conv = sum(conv_k[k] * v_ext[:, k:k + L] for k in range(CONV_W))
    y_conv = jnp.einsum('blc,cd->bld', gate_b * conv, w_conv_out)

    hist = POOL_MAX - 1
    u_ext = jnp.concatenate([pool_hist.astype(u.dtype), u], axis=1)
    csum = jnp.cumsum(u_ext.astype(jnp.float32), axis=1)
    csum = jnp.pad(csum, ((0, 0), (1, 0), (0, 0)))
    upto = csum[:, hist + 1:]
    pos = pos0 + jnp.arange(L, dtype=jnp.int32)
    means = []
    for g, w in enumerate(POOL_WINDOWS):
        sl = slice(g * POOL_GROUP_DIM, (g + 1) * POOL_GROUP_DIM)
        win = upto[..., sl] - csum[:, hist + 1 - w:hist + 1 - w + L, sl]
        cnt = jnp.minimum(pos + 1, w).astype(jnp.float32)[None, :, None]
        means.append(win / cnt)
    pooled = (jnp.concatenate(means, axis=-1) - u.astype(jnp.float32)).astype(u.dtype)
    pooled = pooled.reshape(b, L, POOL_GROUPS, POOL_GROUP_DIM)
    y_pool = jnp.einsum('blgc,gce->blge', pooled, w_pool_map).reshape(b, L, D_MODEL) * pool_scale

    mix = jax.nn.sigmoid(g_a) * y_conv + jax.nn.sigmoid(g_b) * y_pool
    out = jnp.einsum('bld,de->ble', mix, w_o)
    return out, v_ext[:, -(CONV_W - 1):], u_ext[:, -hist:]


def moe_ffn(x, w_router, b_router, w_gate_up, b_gate_up, w_down, b_down):
    T = x.shape[0]
    logits = jnp.einsum('td,de->te', x, w_router).astype(jnp.float32) + b_router.astype(jnp.float32)
    top_vals, top_idx = lax.top_k(logits, TOP_K)
    gate = jax.nn.softmax(top_vals, axis=-1)
    n_slots = T * TOP_K
    slot_e = top_idx.reshape(-1).astype(jnp.int32)
    slot_tok = jnp.arange(n_slots, dtype=jnp.int32) // TOP_K
    order = jnp.argsort(slot_e)
    e_sorted = slot_e[order]
    counts = jnp.bincount(slot_e, length=N_EXPERTS).astype(jnp.int32)
    start = jnp.cumsum(counts) - counts
    padded = (counts + EXPERT_BLOCK - 1) // EXPERT_BLOCK * EXPERT_BLOCK
    pad_end = jnp.cumsum(padded)
    pad_start = pad_end - padded
    dest = pad_start[e_sorted] + jnp.arange(n_slots, dtype=jnp.int32) - start[e_sorted]
    n_blocks = -(-n_slots // EXPERT_BLOCK) + N_EXPERTS
    n_rows = n_blocks * EXPERT_BLOCK
    row_tok = jnp.zeros((n_rows,), jnp.int32).at[dest].set(slot_tok[order])
    row_w = jnp.zeros((n_rows,), jnp.float32).at[dest].set(gate.reshape(-1)[order])
    block_e = jnp.minimum(
        jnp.searchsorted(pad_end, jnp.arange(n_blocks, dtype=jnp.int32) * EXPERT_BLOCK, side='right'),
        N_EXPERTS - 1).astype(jnp.int32)

    def block_step(acc, blk):
        e, tok, w = blk
        xb = x[tok]
        h = xb @ w_gate_up[e] + b_gate_up[e]
        glu = jnp.minimum(h[:, :D_FF], SWIGLU_LIMIT)
        lin = jnp.clip(h[:, D_FF:], -SWIGLU_LIMIT, SWIGLU_LIMIT)
        act = glu * jax.nn.sigmoid(SWIGLU_ALPHA * glu) * (lin + 1.0)
        yb = act @ w_down[e] + b_down[e]
        return acc.at[tok].add(yb.astype(jnp.float32) * w[:, None]), None

    acc, _ = lax.scan(block_step, jnp.zeros((T, D_MODEL), jnp.float32),
                      (block_e, row_tok.reshape(n_blocks, EXPERT_BLOCK), row_w.reshape(n_blocks, EXPERT_BLOCK)))
    return acc.astype(x.dtype)


def _uinit(key, shape, std):
    a = std * math.sqrt(3.0)
    return jax.random.uniform(key, shape, jnp.float32, -a, a)


def setup_inputs(seed: int = 0) -> dict:
    key = jax.random.key(seed)
    ks = jax.random.split(key, 20)
    nrm = jax.random.normal
    return {
        'x_prompt': nrm(ks[0], (BATCH, SEQ, D_MODEL), jnp.float32),
        'x_sample': nrm(ks[1], (DEC_BATCH, DEC_SEQ, D_MODEL), jnp.float32),
        'cache_conv': nrm(ks[2], (DEPTH, DEC_BATCH, CONV_W - 1, D_CONV), jnp.float32),
        'cache_pool': nrm(ks[3], (DEPTH, DEC_BATCH, POOL_MAX - 1, D_POOL), jnp.float32),
        'norm_mix_g': 1.0 + 0.1 * nrm(ks[4], (DEPTH, D_MODEL), jnp.float32),
        'w_in': _uinit(ks[5], (DEPTH, D_MODEL, IN_COLS), D_MODEL ** -0.5),
        'conv_k': _uinit(ks[6], (DEPTH, CONV_W, D_CONV), CONV_W ** -0.5),
        'w_conv_out': _uinit(ks[7], (DEPTH, D_CONV, D_MODEL), D_CONV ** -0.5),
        'w_pool_map': _uinit(ks[8], (DEPTH, POOL_GROUPS, POOL_GROUP_DIM, POOL_OUT_GROUP), POOL_GROUP_DIM ** -0.5),
        'pool_scale': 1.0 + 0.1 * nrm(ks[9], (DEPTH, D_MODEL), jnp.float32),
        'w_o': _uinit(ks[10], (DEPTH, D_MODEL, D_MODEL), D_MODEL ** -0.5),
        'norm_ffn_g': 1.0 + 0.1 * nrm(ks[11], (DEPTH, D_MODEL), jnp.float32),
        'w_router': _uinit(ks[12], (DEPTH, D_MODEL, N_EXPERTS), D_MODEL ** -0.5),
        'b_router': 0.01 * nrm(ks[13], (DEPTH, N_EXPERTS), jnp.float32),
        'w_gate_up': _uinit(ks[14], (DEPTH, N_EXPERTS, D_MODEL, 2 * D_FF), D_MODEL ** -0.5),
        'b_gate_up': 0.01 * nrm(ks[15], (DEPTH, N_EXPERTS, 2 * D_FF), jnp.float32),
        'w_down': _uinit(ks[16], (DEPTH, N_EXPERTS, D_FF, D_MODEL), D_FF ** -0.5),
        'b_down': 0.01 * nrm(ks[17], (DEPTH, N_EXPERTS, D_MODEL), jnp.float32),
        'norm_final_g': 1.0 + 0.1 * nrm(ks[18], (D_MODEL,), jnp.float32),
    }


def reference(x_prompt, x_sample, cache_conv, cache_pool, norm_mix_g, w_in, conv_k, w_conv_out,
              w_pool_map, pool_scale, w_o, norm_ffn_g, w_router, b_router, w_gate_up, b_gate_up,
              w_down, b_down, norm_final_g):
    xp, xs = x_prompt, x_sample
    bp, lp, _ = xp.shape
    bs, ls, _ = xs.shape
    conv_p, pool_p, conv_s, pool_s = [], [], [], []
    for l in range(DEPTH):
        mix_w = (w_in[l], conv_k[l], w_conv_out[l], w_pool_map[l], pool_scale[l], w_o[l])
        zero_conv = jnp.zeros((bp, CONV_W - 1, D_CONV), xp.dtype)
        zero_pool = jnp.zeros((bp, POOL_MAX - 1, D_POOL), xp.dtype)
        a_p, cst_p, pst_p = token_mixer(rms_norm(xp, norm_mix_g[l]), zero_conv, zero_pool, 0, *mix_w)
        a_s, cst_s, pst_s = token_mixer(rms_norm(xs, norm_mix_g[l]), cache_conv[l], cache_pool[l], PAST_LEN, *mix_w)
        xp = xp + a_p
        xs = xs + a_s
        hn = jnp.concatenate([rms_norm(xp, norm_ffn_g[l]).reshape(bp * lp, D_MODEL),
                              rms_norm(xs, norm_ffn_g[l]).reshape(bs * ls, D_MODEL)], axis=0)
        f = moe_ffn(hn, w_router[l], b_router[l], w_gate_up[l], b_gate_up[l], w_down[l], b_down[l])
        xp = xp + f[:bp * lp].reshape(bp, lp, D_MODEL)
        xs = xs + f[bp * lp:].reshape(bs, ls, D_MODEL)
        conv_p.append(cst_p)
        pool_p.append(pst_p)
        conv_s.append(cst_s)
        pool_s.append(pst_s)
    y_prompt = rms_norm(xp, norm_final_g)
    y_sample = rms_norm(xs, norm_final_g)
    return (y_prompt, y_sample, jnp.stack(conv_p), jnp.stack(pool_p), jnp.stack(conv_s), jnp.stack(pool_s))
```

```python
import functools

import jax
import jax.numpy as jnp
from jax import lax
from jax.experimental import pallas as pl
from jax.experimental.pallas import tpu as pltpu

D_MODEL = 4096
BATCH = 4
SEQ = 2048
DEC_BATCH = 16
DEC_SEQ = 64
D_CONV = D_MODEL // 2
D_POOL = D_MODEL // 2
CONV_HIST = 2
POOL_HIST = 15
POOL_GROUP_DIM = D_POOL // 4
N_EXPERTS = 32
TOP_K = 4
D_FF = D_MODEL
SWIGLU_ALPHA = 1.702
SWIGLU_LIMIT = 7.0
EPS = 1e-5

T_P = BATCH * SEQ
T_S = DEC_BATCH * DEC_SEQ
T = T_P + T_S
HALF = D_MODEL // 2

ROW_TILE = 1024
N_ROW_TILES = T // ROW_TILE
N_PROMPT_TILES = T_P // ROW_TILE
TC = 256
CONV_PAD = 8
POOL_PAD = 16

EB = 256
N_SLOTS = T * TOP_K
N_BLOCKS = N_SLOTS // EB + N_EXPERTS
N_ROWS = N_BLOCKS * EB
TN1 = 512
TN2 = 1024
NCOL1 = D_FF // TN1
NCOL2 = D_MODEL // TN2
COMB_TOK = 64

VMEM_LIMIT = 56 * 1024 * 1024


def _cparams(sem):
    return pltpu.CompilerParams(dimension_semantics=sem, vmem_limit_bytes=VMEM_LIMIT)


def _norm_kernel(xp_ref, xs_ref, g_ref, o_ref):
    i = pl.program_id(0)

    def body(x):
        y = x * lax.rsqrt(jnp.mean(x * x, axis=-1, keepdims=True) + EPS)
        o_ref[...] = (y * g_ref[...]).astype(o_ref.dtype)

    @pl.when(i < T_P // 512)
    def _():
        body(xp_ref[...])

    @pl.when(i >= T_P // 512)
    def _():
        body(xs_ref[...])


def _norm(xp, xs, g):
    np_ = T_P // 512
    return pl.pallas_call(
        _norm_kernel,
        grid=(T // 512,),
        in_specs=[
            pl.BlockSpec((512, D_MODEL), lambda i: (jnp.minimum(i, np_ - 1), 0)),
            pl.BlockSpec((512, D_MODEL), lambda i: (jnp.maximum(i - np_, 0), 0)),
            pl.BlockSpec((1, D_MODEL), lambda i: (0, 0)),
        ],
        out_specs=pl.BlockSpec((512, D_MODEL), lambda i: (i, 0)),
        out_shape=jax.ShapeDtypeStruct((T, D_MODEL), jnp.bfloat16),
        compiler_params=_cparams(("arbitrary",)),
        name="mix_norm",
    )(xp, xs, g)


def _conv3(ev, ck):
    return ck[2:3, :] * ev + ck[1:2, :] * pltpu.roll(ev, 1, 0) + ck[0:1, :] * pltpu.roll(ev, 2, 0)


def _window_sum(eu, group):
    s2 = eu + pltpu.roll(eu, 1, 0)
    s4 = s2 + pltpu.roll(s2, 2, 0)
    s8 = s4 + pltpu.roll(s4, 4, 0)
    s16 = s8 + pltpu.roll(s8, 8, 0)
    return jnp.where(group == 0, s2, jnp.where(group == 1, s4, jnp.where(group == 2, s8, s16)))


def _proj_kernel(xn_ref, wb_ref, wc_ref, wx_ref, wu_ref, ck_ref, cc_ref, cp_ref,
                 z_ref, pooled_ref, csp_ref, psp_ref, css_ref, pss_ref,
                 vcarry, ucarry, vs, us):
    c = pl.program_id(0)
    i = pl.program_id(1)
    xn = xn_ref[...]
    dot = functools.partial(jnp.dot, preferred_element_type=jnp.float32)
    gate_b = dot(xn, wb_ref[...])
    v = dot(xn, wc_ref[...]) * dot(xn, wx_ref[...])
    u = dot(xn, wu_ref[...])
    ck = ck_ref[...]
    group = c // (POOL_GROUP_DIM // TC)
    window = jnp.left_shift(2, group)

    @pl.when(i < N_PROMPT_TILES)
    def _prompt():
        first = (i % 2) == 0
        vh = jnp.where(first, 0.0, vcarry[...])
        uh = jnp.where(first, 0.0, ucarry[...])
        conv = _conv3(jnp.concatenate([vh, v], axis=0), ck)[CONV_PAD:]
        z_ref[...] = (gate_b * conv).astype(z_ref.dtype)
        win = _window_sum(jnp.concatenate([uh, u], axis=0), group)[POOL_PAD:]
        pos = (i % 2) * ROW_TILE + lax.broadcasted_iota(jnp.int32, (ROW_TILE, 1), 0)
        cnt = jnp.minimum(pos + 1, window).astype(jnp.float32)
        pooled_ref[...] = (win * (1.0 / cnt) - u).astype(pooled_ref.dtype)
        vcarry[...] = v[ROW_TILE - CONV_PAD:]
        ucarry[...] = u[ROW_TILE - POOL_PAD:]
        csp_ref[0] = vcarry[CONV_PAD - CONV_HIST:, :]
        psp_ref[0] = ucarry[POOL_PAD - POOL_HIST:, :]

    @pl.when(i >= N_PROMPT_TILES)
    def _sample():
        ev, eu = [], []
        for s in range(DEC_BATCH):
            ev += [cc_ref[s], v[s * DEC_SEQ:(s + 1) * DEC_SEQ]]
            eu += [cp_ref[s], u[s * DEC_SEQ:(s + 1) * DEC_SEQ]]
        conv_e = _conv3(jnp.concatenate(ev, axis=0), ck)
        win_e = _window_sum(jnp.concatenate(eu, axis=0), group)
        lv, lu = CONV_PAD + DEC_SEQ, POOL_PAD + DEC_SEQ
        conv = jnp.concatenate([conv_e[s * lv + CONV_PAD:(s + 1) * lv] for s in range(DEC_BATCH)], axis=0)
        win = jnp.concatenate([win_e[s * lu + POOL_PAD:(s + 1) * lu] for s in range(DEC_BATCH)], axis=0)
        z_ref[...] = (gate_b * conv).astype(z_ref.dtype)
        inv = 1.0 / window.astype(jnp.float32)
        pooled_ref[...] = (win * inv - u).astype(pooled_ref.dtype)
        vs[...] = v
        us[...] = u
        for s in range(DEC_BATCH):
            end = (s + 1) * DEC_SEQ
            css_ref[s] = vs[end - CONV_HIST:end, :]
            pss_ref[s] = us[end - POOL_HIST:end, :]


def _proj(xn, w_in_bf, conv_k, cache_conv_pad, cache_pool_pad):
    nsec = D_CONV // TC
    pidx = lambda c, i: (jnp.minimum(i, N_PROMPT_TILES - 1) // 2, 0, c)
    return pl.pallas_call(
        _proj_kernel,
        grid=(nsec, N_ROW_TILES),
        in_specs=[
            pl.BlockSpec((ROW_TILE, D_MODEL), lambda c, i: (i, 0)),
            pl.BlockSpec((D_MODEL, TC), lambda c, i: (0, c)),
            pl.BlockSpec((D_MODEL, TC), lambda c, i: (0, nsec + c)),
            pl.BlockSpec((D_MODEL, TC), lambda c, i: (0, 2 * nsec + c)),
            pl.BlockSpec((D_MODEL, TC), lambda c, i: (0, 3 * nsec + c)),
            pl.BlockSpec((3, TC), lambda c, i: (0, c)),
            pl.BlockSpec((DEC_BATCH, CONV_PAD, TC), lambda c, i: (0, 0, c)),
            pl.BlockSpec((DEC_BATCH, POOL_PAD, TC), lambda c, i: (0, 0, c)),
        ],
        out_specs=[
            pl.BlockSpec((ROW_TILE, TC), lambda c, i: (i, c)),
            pl.BlockSpec((ROW_TILE, TC), lambda c, i: (i, c)),
            pl.BlockSpec((1, CONV_HIST, TC), pidx),
            pl.BlockSpec((1, POOL_HIST, TC), pidx),
            pl.BlockSpec((DEC_BATCH, CONV_HIST, TC), lambda c, i: (0, 0, c)),
            pl.BlockSpec((DEC_BATCH, POOL_HIST, TC), lambda c, i: (0, 0, c)),
        ],
        out_shape=[
            jax.ShapeDtypeStruct((T, D_CONV), jnp.bfloat16),
            jax.ShapeDtypeStruct((T, D_POOL), jnp.bfloat16),
            jax.ShapeDtypeStruct((BATCH, CONV_HIST, D_CONV), jnp.float32),
            jax.ShapeDtypeStruct((BATCH, POOL_HIST, D_POOL), jnp.float32),
            jax.ShapeDtypeStruct((DEC_BATCH, CONV_HIST, D_CONV), jnp.float32),
            jax.ShapeDtypeStruct((DEC_BATCH, POOL_HIST, D_POOL), jnp.float32),
        ],
        scratch_shapes=[
            pltpu.VMEM((CONV_PAD, TC), jnp.float32),
            pltpu.VMEM((POOL_PAD, TC), jnp.float32),
            pltpu.VMEM((ROW_TILE, TC), jnp.float32),
            pltpu.VMEM((ROW_TILE, TC), jnp.float32),
        ],
        compiler_params=_cparams(("arbitrary", "arbitrary")),
        name="mix_proj",
    )(xn, w_in_bf, w_in_bf, w_in_bf, w_in_bf, conv_k, cache_conv_pad, cache_pool_pad)


def _gates_kernel(xn_ref, w_ref, o_ref):
    g = jnp.dot(xn_ref[...], w_ref[...], preferred_element_type=jnp.float32)
    o_ref[...] = jax.nn.sigmoid(g).astype(o_ref.dtype)


def _gates(xn, w_in_bf):
    tn = 1024
    off = (3 * D_CONV + D_POOL) // tn
    return pl.pallas_call(
        _gates_kernel,
        grid=(2 * D_MODEL // tn, N_ROW_TILES),
        in_specs=[
            pl.BlockSpec((ROW_TILE, D_MODEL), lambda j, i: (i, 0)),
            pl.BlockSpec((D_MODEL, tn), lambda j, i: (0, off + j)),
        ],
        out_specs=pl.BlockSpec((ROW_TILE, tn), lambda j, i: (i, j)),
        out_shape=jax.ShapeDtypeStruct((T, 2 * D_MODEL), jnp.bfloat16),
        compiler_params=_cparams(("arbitrary", "arbitrary")),
        name="mix_gates",
    )(xn, w_in_bf)


def _merge_kernel(z_ref, wco_ref, p_ref, wpm_ref, scale_ref, ga_ref, gb_ref, o_ref):
    y_conv = jnp.dot(z_ref[...], wco_ref[...], preferred_element_type=jnp.float32)
    y_pool = jnp.dot(p_ref[...], wpm_ref[0], preferred_element_type=jnp.float32) * scale_ref[...]
    mix = ga_ref[...].astype(jnp.float32) * y_conv + gb_ref[...].astype(jnp.float32) * y_pool
    o_ref[...] = mix.astype(o_ref.dtype)


def _merge(z, wco_bf, pooled, wpm_bf, pool_scale, sg):
    tn = D_MODEL // 4
    return pl.pallas_call(
        _merge_kernel,
        grid=(4, N_ROW_TILES),
        in_specs=[
            pl.BlockSpec((ROW_TILE, D_CONV), lambda j, i: (i, 0)),
            pl.BlockSpec((D_CONV, tn), lambda j, i: (0, j)),
            pl.BlockSpec((ROW_TILE, POOL_GROUP_DIM), lambda j, i: (i, j)),
            pl.BlockSpec((1, POOL_GROUP_DIM, tn), lambda j, i: (j, 0, 0)),
            pl.BlockSpec((1, tn), lambda j, i: (0, j)),
            pl.BlockSpec((ROW_TILE, tn), lambda j, i: (i, j)),
            pl.BlockSpec((ROW_TILE, tn), lambda j, i: (i, 4 + j)),
        ],
        out_specs=pl.BlockSpec((ROW_TILE, tn), lambda j, i: (i, j)),
        out_shape=jax.ShapeDtypeStruct((T, D_MODEL), jnp.bfloat16),
        compiler_params=_cparams(("arbitrary", "arbitrary")),
        name="mix_merge",
    )(z, wco_bf, pooled, wpm_bf, pool_scale, sg, sg)


def _oproj_kernel(m_ref, w_ref, xp_ref, xs_ref, o_ref):
    i = pl.program_id(1)
    a = jnp.dot(m_ref[...], w_ref[...], preferred_element_type=jnp.float32)

    @pl.when(i < T_P // 512)
    def _():
        o_ref[...] = xp_ref[...] + a

    @pl.when(i >= T_P // 512)
    def _():
        o_ref[...] = xs_ref[...] + a


def _oproj(mix, w_o_bf, xp, xs):
    tn = 1024
    np_ = T_P // 512
    return pl.pallas_call(
        _oproj_kernel,
        grid=(D_MODEL // tn, T // 512),
        in_specs=[
            pl.BlockSpec((512, D_MODEL), lambda j, i: (i, 0)),
            pl.BlockSpec((D_MODEL, tn), lambda j, i: (0, j)),
            pl.BlockSpec((512, tn), lambda j, i: (jnp.minimum(i, np_ - 1), j)),
            pl.BlockSpec((512, tn), lambda j, i: (jnp.maximum(i - np_, 0), j)),
        ],
        out_specs=pl.BlockSpec((512, tn), lambda j, i: (i, j)),
        out_shape=jax.ShapeDtypeStruct((T, D_MODEL), jnp.float32),
        compiler_params=_cparams(("arbitrary", "arbitrary")),
        name="mix_oproj",
    )(mix, w_o_bf, xp, xs)


def _f32_bits(x):
    return pltpu.bitcast(x, jnp.uint32)


def _route_kernel(h_ref, g_ref, wr_ref, br_ref, hn_ref, idx_ref, gate_ref):
    h = h_ref[...]
    hn = h * lax.rsqrt(jnp.mean(h * h, axis=-1, keepdims=True) + EPS) * g_ref[...]
    lo = _f32_bits(hn[:, :HALF].astype(jnp.bfloat16).astype(jnp.float32))
    hi = _f32_bits(hn[:, HALF:].astype(jnp.bfloat16).astype(jnp.float32))
    hn_ref[...] = jnp.right_shift(lo, jnp.uint32(16)) | (hi & jnp.uint32(0xFFFF0000))
    logits = lax.dot_general(wr_ref[...], hn, (((1,), (1,)), ((), ())),
                             precision=lax.Precision.HIGHEST,
                             preferred_element_type=jnp.float32) + br_ref[...]
    eid = lax.broadcasted_iota(jnp.int32, logits.shape, 0)
    vals, ids = [], []
    for _ in range(TOP_K):
        m = jnp.max(logits, axis=0, keepdims=True)
        sel = jnp.min(jnp.where(logits == m, eid, N_EXPERTS), axis=0, keepdims=True)
        vals.append(m)
        ids.append(sel)
        logits = jnp.where(eid == sel, -jnp.inf, logits)
    ex = [jnp.exp(v - vals[0]) for v in vals]
    den = ex[0] + ex[1] + ex[2] + ex[3]
    for k in range(TOP_K):
        idx_ref[k:k + 1, :] = ids[k]
        gate_ref[k:k + 1, :] = ex[k] / den


def _route(h, g, wr_t, br):
    tm = 512
    return pl.pallas_call(
        _route_kernel,
        grid=(T // tm,),
        in_specs=[
            pl.BlockSpec((tm, D_MODEL), lambda i: (i, 0)),
            pl.BlockSpec((1, D_MODEL), lambda i: (0, 0)),
            pl.BlockSpec((N_EXPERTS, D_MODEL), lambda i: (0, 0)),
            pl.BlockSpec((N_EXPERTS, 1), lambda i: (0, 0)),
        ],
        out_specs=[
            pl.BlockSpec((tm, HALF), lambda i: (i, 0)),
            pl.BlockSpec((TOP_K, tm), lambda i: (0, i)),
            pl.BlockSpec((TOP_K, tm), lambda i: (0, i)),
        ],
        out_shape=[
            jax.ShapeDtypeStruct((T, HALF), jnp.uint32),
            jax.ShapeDtypeStruct((TOP_K, T), jnp.int32),
            jax.ShapeDtypeStruct((TOP_K, T), jnp.float32),
        ],
        compiler_params=_cparams(("arbitrary",)),
        name="ffn_route",
    )(h, g, wr_t, br)


def _gather_kernel(tok_ref, hn_hbm, o_ref, sem):
    b = pl.program_id(0)

    def row_copy(r):
        tok = tok_ref[b * EB + r]
        return pltpu.make_async_copy(hn_hbm.at[pl.ds(tok, 1), :], o_ref.at[pl.ds(r, 1), :], sem)

    def start(r, carry):
        row_copy(r).start()
        return carry

    def wait(r, carry):
        row_copy(r).wait()
        return carry

    lax.fori_loop(0, EB, start, 0)
    lax.fori_loop(0, EB, wait, 0)


def _gather(row_tok, hn_packed):
    return pl.pallas_call(
        _gather_kernel,
        grid_spec=pltpu.PrefetchScalarGridSpec(
            num_scalar_prefetch=1,
            grid=(N_BLOCKS,),
            in_specs=[pl.BlockSpec(memory_space=pl.ANY)],
            out_specs=pl.BlockSpec((EB, HALF), lambda b, tok: (b, 0)),
            scratch_shapes=[pltpu.SemaphoreType.DMA(())],
        ),
        out_shape=jax.ShapeDtypeStruct((N_ROWS, HALF), jnp.uint32),
        compiler_params=_cparams(("arbitrary",)),
        name="ffn_gather",
    )(row_tok, hn_packed)


def _unpack_rows(xw):
    lo = pltpu.bitcast(jnp.left_shift(xw, jnp.uint32(16)), jnp.float32).astype(jnp.bfloat16)
    hi = pltpu.bitcast(xw & jnp.uint32(0xFFFF0000), jnp.float32).astype(jnp.bfloat16)
    return lo, hi


def _gemm1_kernel(se_ref, sj_ref, sb_ref, so_ref, sf_ref, sv_ref,
                  x_ref, wg_ref, wl_ref, bg_ref, bl_ref, o_ref, wg_bf, wl_bf):
    s = pl.program_id(0)

    @pl.when(sv_ref[s] == 0)
    def _():
        o_ref[...] = jnp.zeros_like(o_ref)

    @pl.when(sf_ref[s] == 1)
    def _():
        wg_bf[...] = wg_ref[0].astype(jnp.bfloat16)
        wl_bf[...] = wl_ref[0].astype(jnp.bfloat16)

    @pl.when(sv_ref[s] == 1)
    def _():
        lo, hi = _unpack_rows(x_ref[...])
        dot = functools.partial(jnp.dot, preferred_element_type=jnp.float32)
        hg = dot(lo, wg_bf[:HALF, :]) + dot(hi, wg_bf[HALF:, :]) + bg_ref[0]
        hl = dot(lo, wl_bf[:HALF, :]) + dot(hi, wl_bf[HALF:, :]) + bl_ref[0]
        glu = jnp.minimum(hg, SWIGLU_LIMIT)
        lin = jnp.clip(hl, -SWIGLU_LIMIT, SWIGLU_LIMIT)
        act = glu * jax.nn.sigmoid(SWIGLU_ALPHA * glu) * (lin + 1.0)
        o_ref[...] = act.astype(o_ref.dtype)


def _gemm1(tables, x_sorted, w_gate_up, b_gate_up):
    n_steps = N_BLOCKS * NCOL1
    return pl.pallas_call(
        _gemm1_kernel,
        grid_spec=pltpu.PrefetchScalarGridSpec(
            num_scalar_prefetch=6,
            grid=(n_steps,),
            in_specs=[
                pl.BlockSpec((EB, HALF), lambda s, se, sj, sb, so, sf, sv: (sb[s], 0)),
                pl.BlockSpec((1, D_MODEL, TN1), lambda s, se, sj, sb, so, sf, sv: (se[s], 0, sj[s])),
                pl.BlockSpec((1, D_MODEL, TN1), lambda s, se, sj, sb, so, sf, sv: (se[s], 0, NCOL1 + sj[s])),
                pl.BlockSpec((1, 1, TN1), lambda s, se, sj, sb, so, sf, sv: (se[s], 0, sj[s])),
                pl.BlockSpec((1, 1, TN1), lambda s, se, sj, sb, so, sf, sv: (se[s], 0, NCOL1 + sj[s])),
            ],
            out_specs=pl.BlockSpec((EB, TN1), lambda s, se, sj, sb, so, sf, sv: (sb[s], so[s])),
            scratch_shapes=[
                pltpu.VMEM((D_MODEL, TN1), jnp.bfloat16),
                pltpu.VMEM((D_MODEL, TN1), jnp.bfloat16),
            ],
        ),
        out_shape=jax.ShapeDtypeStruct((N_ROWS, D_FF), jnp.bfloat16),
        compiler_params=_cparams(("arbitrary",)),
        name="ffn_gate_up",
    )(*tables, x_sorted, w_gate_up, w_gate_up, b_gate_up, b_gate_up)


def _gemm2_kernel(se_ref, sj_ref, sb_ref, so_ref, sf_ref, sv_ref, a_ref, w_ref, b_ref, o_ref, w_bf):
    s = pl.program_id(0)

    @pl.when(sv_ref[s] == 0)
    def _():
        o_ref[...] = jnp.zeros_like(o_ref)

    @pl.when(sf_ref[s] == 1)
    def _():
        w_bf[...] = w_ref[0].astype(jnp.bfloat16)

    @pl.when(sv_ref[s] == 1)
    def _():
        o_ref[...] = jnp.dot(a_ref[...], w_bf[...], preferred_element_type=jnp.float32) + b_ref[0]


def _gemm2(tables, act, w_down, b_down):
    n_steps = N_BLOCKS * NCOL2
    return pl.pallas_call(
        _gemm2_kernel,
        grid_spec=pltpu.PrefetchScalarGridSpec(
            num_scalar_prefetch=6,
            grid=(n_steps,),
            in_specs=[
                pl.BlockSpec((EB, D_FF), lambda s, se, sj, sb, so, sf, sv: (sb[s], 0)),
                pl.BlockSpec((1, D_FF, TN2), lambda s, se, sj, sb, so, sf, sv: (se[s], 0, sj[s])),
                pl.BlockSpec((1, 1, TN2), lambda s, se, sj, sb, so, sf, sv: (se[s], 0, sj[s])),
            ],
            out_specs=pl.BlockSpec((EB, TN2), lambda s, se, sj, sb, so, sf, sv: (sb[s], so[s])),
            scratch_shapes=[pltpu.VMEM((D_FF, TN2), jnp.bfloat16)],
        ),
        out_shape=jax.ShapeDtypeStruct((N_ROWS, D_MODEL), jnp.float32),
        compiler_params=_cparams(("arbitrary",)),
        name="ffn_down",
    )(*tables, act, w_down, b_down)


def _combine_kernel(pos_ref, h_ref, gate_ref, g_ref, y_hbm, o_ref, rows, sem, *, tok0):
    i = pl.program_id(0)
    base = tok0 + i * COMB_TOK

    def row_copy(k, t):
        p = pos_ref[k * T + base + t]
        return pltpu.make_async_copy(y_hbm.at[pl.ds(p, 1), :], rows.at[k, pl.ds(t, 1), :], sem)

    def start(t, carry):
        for k in range(TOP_K):
            row_copy(k, t).start()
        return carry

    def wait(t, carry):
        for k in range(TOP_K):
            row_copy(k, t).wait()
        return carry

    lax.fori_loop(0, COMB_TOK, start, 0)
    lax.fori_loop(0, COMB_TOK, wait, 0)
    gate = gate_ref[...]
    acc = rows[0] * gate[:, 0:1]
    for k in range(1, TOP_K):
        acc = acc + rows[k] * gate[:, k:k + 1]
    x = h_ref[...] + acc
    y = x * lax.rsqrt(jnp.mean(x * x, axis=-1, keepdims=True) + EPS)
    o_ref[...] = y * g_ref[...]


def _combine(pos_flat, h, gate_tk, g_final, y_sorted, tok0, n_tok):
    blk0 = tok0 // COMB_TOK
    return pl.pallas_call(
        functools.partial(_combine_kernel, tok0=tok0),
        grid_spec=pltpu.PrefetchScalarGridSpec(
            num_scalar_prefetch=1,
            grid=(n_tok // COMB_TOK,),
            in_specs=[
                pl.BlockSpec((COMB_TOK, D_MODEL), lambda i, pos: (blk0 + i, 0)),
                pl.BlockSpec((COMB_TOK, TOP_K), lambda i, pos: (blk0 + i, 0)),
                pl.BlockSpec((1, D_MODEL), lambda i, pos: (0, 0)),
                pl.BlockSpec(memory_space=pl.ANY),
            ],
            out_specs=pl.BlockSpec((COMB_TOK, D_MODEL), lambda i, pos: (i, 0)),
            scratch_shapes=[
                pltpu.VMEM((TOP_K, COMB_TOK, D_MODEL), jnp.float32),
                pltpu.SemaphoreType.DMA(()),
            ],
        ),
        out_shape=jax.ShapeDtypeStruct((n_tok, D_MODEL), jnp.float32),
        compiler_params=_cparams(("arbitrary",)),
        name="ffn_combine",
    )(pos_flat, h, gate_tk, g_final, y_sorted)


def _routing(idx_t):
    e_flat = idx_t.reshape(-1)
    onehot = (e_flat[:, None] == jnp.arange(N_EXPERTS, dtype=jnp.int32)[None, :]).astype(jnp.int32)
    csum = jnp.cumsum(onehot, axis=0)
    rank = jnp.take_along_axis(csum, e_flat[:, None], axis=1)[:, 0] - 1
    counts = csum[-1]
    nblk = (counts + EB - 1) // EB
    blk_end = jnp.cumsum(nblk)
    blk_start = blk_end - nblk
    dest = blk_start[e_flat] * EB + rank
    tok = jnp.tile(jnp.arange(T, dtype=jnp.int32), TOP_K)
    row_tok = jnp.zeros((N_ROWS,), jnp.int32).at[dest].set(tok)

    def tables(ncol):
        n_steps = N_BLOCKS * ncol
        step_end = blk_end * ncol
        total = step_end[-1]
        s = jnp.arange(n_steps, dtype=jnp.int32)
        sc = jnp.minimum(s, total - 1)
        e = jnp.minimum(jnp.searchsorted(step_end, sc, side='right'), N_EXPERTS - 1).astype(jnp.int32)
        local = sc - (step_end[e] - nblk[e] * ncol)
        j = local // nblk[e]
        i = local % nblk[e]
        valid = s < total
        first = valid & (i == 0)
        spare = s - total
        b = jnp.where(valid, blk_start[e] + i, blk_end[-1] + spare // ncol)
        jo = jnp.where(valid, j, spare % ncol)
        return (e, j.astype(jnp.int32), b.astype(jnp.int32), jo.astype(jnp.int32),
                first.astype(jnp.int32), valid.astype(jnp.int32))

    return row_tok, dest.astype(jnp.int32), tables(NCOL1), tables(NCOL2)


def kernel(x_prompt, x_sample, cache_conv, cache_pool, norm_mix_g, w_in, conv_k, w_conv_out, w_pool_map,
           pool_scale, w_o, norm_ffn_g, w_router, b_router, w_gate_up, b_gate_up, w_down, b_down,
           norm_final_g):
    bf = jnp.bfloat16
    xp = x_prompt.reshape(T_P, D_MODEL)
    xs = x_sample.reshape(T_S, D_MODEL)
    cc_pad = jnp.pad(cache_conv[0], ((0, 0), (CONV_PAD - CONV_HIST, 0), (0, 0)))
    cp_pad = jnp.pad(cache_pool[0], ((0, 0), (POOL_PAD - POOL_HIST, 0), (0, 0)))

    xn = _norm(xp, xs, norm_mix_g)
    w_in_bf = w_in[0].astype(bf)
    z, pooled, cs_p, ps_p, cs_s, ps_s = _proj(xn, w_in_bf, conv_k[0], cc_pad, cp_pad)
    sg = _gates(xn, w_in_bf)
    mix = _merge(z, w_conv_out[0].astype(bf), pooled, w_pool_map[0].astype(bf), pool_scale, sg)
    h = _oproj(mix, w_o[0].astype(bf), xp, xs)

    hn_packed, idx_t, gate_t = _route(h, norm_ffn_g, w_router[0].T, b_router[0].reshape(N_EXPERTS, 1))
    row_tok, dest, tables1, tables2 = _routing(idx_t)
    x_sorted = _gather(row_tok, hn_packed)
    act = _gemm1(tables1, x_sorted, w_gate_up[0], b_gate_up[0].reshape(N_EXPERTS, 1, 2 * D_FF))
    y_sorted = _gemm2(tables2, act, w_down[0], b_down[0].reshape(N_EXPERTS, 1, D_MODEL))

    gate_tk = gate_t.T
    g_final = norm_final_g.reshape(1, D_MODEL)
    y_p = _combine(dest, h, gate_tk, g_final, y_sorted, 0, T_P)
    y_s = _combine(dest, h, gate_tk, g_final, y_sorted, T_P, T_S)
    return (y_p.reshape(BATCH, SEQ, D_MODEL), y_s.reshape(DEC_BATCH, DEC_SEQ, D_MODEL),
            cs_p[None], ps_p[None], cs_s[None], ps_s[None])
```

```python
import functools

import jax
import jax.numpy as jnp
from jax import lax
from jax.experimental import pallas as pl
from jax.experimental.pallas import tpu as pltpu

D_MODEL = 4096
BATCH = 4
SEQ = 2048
DEC_BATCH = 16
DEC_SEQ = 64
D_CONV = D_MODEL // 2
D_POOL = D_MODEL // 2
CONV_HIST = 2
POOL_HIST = 15
POOL_GROUP_DIM = D_POOL // 4
N_EXPERTS = 32
TOP_K = 4
D_FF = D_MODEL
SWIGLU_ALPHA = 1.702
SWIGLU_LIMIT = 7.0
EPS = 1e-5

T_P = BATCH * SEQ
T_S = DEC_BATCH * DEC_SEQ
T = T_P + T_S
HALF = D_MODEL // 2

ROW_TILE = 1024
N_ROW_TILES = T // ROW_TILE
N_PROMPT_TILES = T_P // ROW_TILE
TC = 256
CONV_PAD = 8
POOL_PAD = 16

EB = 256
N_SLOTS = T * TOP_K
N_BLOCKS = N_SLOTS // EB + N_EXPERTS
N_ROWS = N_BLOCKS * EB
UNIT_BLOCKS = 6
UNIT_ROWS = UNIT_BLOCKS * EB
N_UNITS = N_EXPERTS + N_BLOCKS // UNIT_BLOCKS
TW = 256
NT1 = D_FF // TW
NT2 = D_MODEL // (2 * TW)
ROUTE_TILE = 512
COMB_TOK = 64

VMEM_LIMIT = 56 * 1024 * 1024


def _cparams(sem):
    return pltpu.CompilerParams(dimension_semantics=sem, vmem_limit_bytes=VMEM_LIMIT)


def _norm_kernel(xp_ref, xs_ref, g_ref, o_ref):
    i = pl.program_id(0)

    def body(x):
        y = x * lax.rsqrt(jnp.mean(x * x, axis=-1, keepdims=True) + EPS)
        o_ref[...] = (y * g_ref[...]).astype(o_ref.dtype)

    @pl.when(i < T_P // 512)
    def _():
        body(xp_ref[...])

    @pl.when(i >= T_P // 512)
    def _():
        body(xs_ref[...])


def _norm(xp, xs, g):
    np_ = T_P // 512
    return pl.pallas_call(
        _norm_kernel,
        grid=(T // 512,),
        in_specs=[
            pl.BlockSpec((512, D_MODEL), lambda i: (jnp.minimum(i, np_ - 1), 0)),
            pl.BlockSpec((512, D_MODEL), lambda i: (jnp.maximum(i - np_, 0), 0)),
            pl.BlockSpec((1, D_MODEL), lambda i: (0, 0)),
        ],
        out_specs=pl.BlockSpec((512, D_MODEL), lambda i: (i, 0)),
        out_shape=jax.ShapeDtypeStruct((T, D_MODEL), jnp.bfloat16),
        compiler_params=_cparams(("arbitrary",)),
        name="mix_norm",
    )(xp, xs, g)


def _conv3(ev, ck):
    return ck[2:3, :] * ev + ck[1:2, :] * pltpu.roll(ev, 1, 0) + ck[0:1, :] * pltpu.roll(ev, 2, 0)


def _window_sum(eu, group):
    s2 = eu + pltpu.roll(eu, 1, 0)
    s4 = s2 + pltpu.roll(s2, 2, 0)
    s8 = s4 + pltpu.roll(s4, 4, 0)
    s16 = s8 + pltpu.roll(s8, 8, 0)
    return jnp.where(group == 0, s2, jnp.where(group == 1, s4, jnp.where(group == 2, s8, s16)))


def _proj_kernel(xn_ref, wb_ref, wc_ref, wx_ref, wu_ref, ck_ref, cc_ref, cp_ref,
                 z_ref, pooled_ref, csp_ref, psp_ref, css_ref, pss_ref,
                 vcarry, ucarry, vs, us):
    c = pl.program_id(0)
    i = pl.program_id(1)
    xn = xn_ref[...]
    dot = functools.partial(jnp.dot, preferred_element_type=jnp.float32)
    gate_b = dot(xn, wb_ref[...])
    v = dot(xn, wc_ref[...]) * dot(xn, wx_ref[...])
    u = dot(xn, wu_ref[...])
    ck = ck_ref[...]
    group = c // (POOL_GROUP_DIM // TC)
    window = jnp.left_shift(2, group)

    @pl.when(i < N_PROMPT_TILES)
    def _prompt():
        first = (i % 2) == 0
        vh = jnp.where(first, 0.0, vcarry[...])
        uh = jnp.where(first, 0.0, ucarry[...])
        conv = _conv3(jnp.concatenate([vh, v], axis=0), ck)[CONV_PAD:]
        z_ref[...] = (gate_b * conv).astype(z_ref.dtype)
        win = _window_sum(jnp.concatenate([uh, u], axis=0), group)[POOL_PAD:]
        pos = (i % 2) * ROW_TILE + lax.broadcasted_iota(jnp.int32, (ROW_TILE, 1), 0)
        cnt = jnp.minimum(pos + 1, window).astype(jnp.float32)
        pooled_ref[...] = (win * (1.0 / cnt) - u).astype(pooled_ref.dtype)
        vcarry[...] = v[ROW_TILE - CONV_PAD:]
        ucarry[...] = u[ROW_TILE - POOL_PAD:]
        csp_ref[0] = vcarry[CONV_PAD - CONV_HIST:, :]
        psp_ref[0] = ucarry[POOL_PAD - POOL_HIST:, :]

    @pl.when(i >= N_PROMPT_TILES)
    def _sample():
        ev, eu = [], []
        for s in range(DEC_BATCH):
            ev += [cc_ref[s], v[s * DEC_SEQ:(s + 1) * DEC_SEQ]]
            eu += [cp_ref[s], u[s * DEC_SEQ:(s + 1) * DEC_SEQ]]
        conv_e = _conv3(jnp.concatenate(ev, axis=0), ck)
        win_e = _window_sum(jnp.concatenate(eu, axis=0), group)
        lv, lu = CONV_PAD + DEC_SEQ, POOL_PAD + DEC_SEQ
        conv = jnp.concatenate([conv_e[s * lv + CONV_PAD:(s + 1) * lv] for s in range(DEC_BATCH)], axis=0)
        win = jnp.concatenate([win_e[s * lu + POOL_PAD:(s + 1) * lu] for s in range(DEC_BATCH)], axis=0)
        z_ref[...] = (gate_b * conv).astype(z_ref.dtype)
        inv = 1.0 / window.astype(jnp.float32)
        pooled_ref[...] = (win * inv - u).astype(pooled_ref.dtype)
        vs[...] = v
        us[...] = u
        for s in range(DEC_BATCH):
            end = (s + 1) * DEC_SEQ
            css_ref[s] = vs[end - CONV_HIST:end, :]
            pss_ref[s] = us[end - POOL_HIST:end, :]


def _proj(xn, w_in_bf, conv_k, cache_conv_pad, cache_pool_pad):
    nsec = D_CONV // TC
    pidx = lambda c, i: (jnp.minimum(i, N_PROMPT_TILES - 1) // 2, 0, c)
    return pl.pallas_call(
        _proj_kernel,
        grid=(nsec, N_ROW_TILES),
        in_specs=[
            pl.BlockSpec((ROW_TILE, D_MODEL), lambda c, i: (i, 0)),
            pl.BlockSpec((D_MODEL, TC), lambda c, i: (0, c)),
            pl.BlockSpec((D_MODEL, TC), lambda c, i: (0, nsec + c)),
            pl.BlockSpec((D_MODEL, TC), lambda c, i: (0, 2 * nsec + c)),
            pl.BlockSpec((D_MODEL, TC), lambda c, i: (0, 3 * nsec + c)),
            pl.BlockSpec((3, TC), lambda c, i: (0, c)),
            pl.BlockSpec((DEC_BATCH, CONV_PAD, TC), lambda c, i: (0, 0, c)),
            pl.BlockSpec((DEC_BATCH, POOL_PAD, TC), lambda c, i: (0, 0, c)),
        ],
        out_specs=[
            pl.BlockSpec((ROW_TILE, TC), lambda c, i: (i, c)),
            pl.BlockSpec((ROW_TILE, TC), lambda c, i: (i, c)),
            pl.BlockSpec((1, CONV_HIST, TC), pidx),
            pl.BlockSpec((1, POOL_HIST, TC), pidx),
            pl.BlockSpec((DEC_BATCH, CONV_HIST, TC), lambda c, i: (0, 0, c)),
            pl.BlockSpec((DEC_BATCH, POOL_HIST, TC), lambda c, i: (0, 0, c)),
        ],
        out_shape=[
            jax.ShapeDtypeStruct((T, D_CONV), jnp.bfloat16),
            jax.ShapeDtypeStruct((T, D_POOL), jnp.bfloat16),
            jax.ShapeDtypeStruct((BATCH, CONV_HIST, D_CONV), jnp.float32),
            jax.ShapeDtypeStruct((BATCH, POOL_HIST, D_POOL), jnp.float32),
            jax.ShapeDtypeStruct((DEC_BATCH, CONV_HIST, D_CONV), jnp.float32),
            jax.ShapeDtypeStruct((DEC_BATCH, POOL_HIST, D_POOL), jnp.float32),
        ],
        scratch_shapes=[
            pltpu.VMEM((CONV_PAD, TC), jnp.float32),
            pltpu.VMEM((POOL_PAD, TC), jnp.float32),
            pltpu.VMEM((ROW_TILE, TC), jnp.float32),
            pltpu.VMEM((ROW_TILE, TC), jnp.float32),
        ],
        compiler_params=_cparams(("arbitrary", "arbitrary")),
        name="mix_proj",
    )(xn, w_in_bf, w_in_bf, w_in_bf, w_in_bf, conv_k, cache_conv_pad, cache_pool_pad)


def _gates_kernel(xn_ref, w_ref, o_ref):
    g = jnp.dot(xn_ref[...], w_ref[...], preferred_element_type=jnp.float32)
    o_ref[...] = jax.nn.sigmoid(g).astype(o_ref.dtype)


def _gates(xn, w_in_bf):
    tn = 1024
    off = (3 * D_CONV + D_POOL) // tn
    return pl.pallas_call(
        _gates_kernel,
        grid=(2 * D_MODEL // tn, N_ROW_TILES),
        in_specs=[
            pl.BlockSpec((ROW_TILE, D_MODEL), lambda j, i: (i, 0)),
            pl.BlockSpec((D_MODEL, tn), lambda j, i: (0, off + j)),
        ],
        out_specs=pl.BlockSpec((ROW_TILE, tn), lambda j, i: (i, j)),
        out_shape=jax.ShapeDtypeStruct((T, 2 * D_MODEL), jnp.bfloat16),
        compiler_params=_cparams(("arbitrary", "arbitrary")),
        name="mix_gates",
    )(xn, w_in_bf)


def _merge_kernel(z_ref, wco_ref, p_ref, wpm_ref, scale_ref, ga_ref, gb_ref, o_ref):
    y_conv = jnp.dot(z_ref[...], wco_ref[...], preferred_element_type=jnp.float32)
    y_pool = jnp.dot(p_ref[...], wpm_ref[0], preferred_element_type=jnp.float32) * scale_ref[...]
    mix = ga_ref[...].astype(jnp.float32) * y_conv + gb_ref[...].astype(jnp.float32) * y_pool
    o_ref[...] = mix.astype(o_ref.dtype)


def _merge(z, wco_bf, pooled, wpm_bf, pool_scale, sg):
    tn = D_MODEL // 4
    return pl.pallas_call(
        _merge_kernel,
        grid=(4, N_ROW_TILES),
        in_specs=[
            pl.BlockSpec((ROW_TILE, D_CONV), lambda j, i: (i, 0)),
            pl.BlockSpec((D_CONV, tn), lambda j, i: (0, j)),
            pl.BlockSpec((ROW_TILE, POOL_GROUP_DIM), lambda j, i: (i, j)),
            pl.BlockSpec((1, POOL_GROUP_DIM, tn), lambda j, i: (j, 0, 0)),
            pl.BlockSpec((1, tn), lambda j, i: (0, j)),
            pl.BlockSpec((ROW_TILE, tn), lambda j, i: (i, j)),
            pl.BlockSpec((ROW_TILE, tn), lambda j, i: (i, 4 + j)),
        ],
        out_specs=pl.BlockSpec((ROW_TILE, tn), lambda j, i: (i, j)),
        out_shape=jax.ShapeDtypeStruct((T, D_MODEL), jnp.bfloat16),
        compiler_params=_cparams(("arbitrary", "arbitrary")),
        name="mix_merge",
    )(z, wco_bf, pooled, wpm_bf, pool_scale, sg, sg)


def _oproj_kernel(m_ref, w_ref, xp_ref, xs_ref, o_ref):
    i = pl.program_id(1)
    a = jnp.dot(m_ref[...], w_ref[...], preferred_element_type=jnp.float32)

    @pl.when(i < T_P // 512)
    def _():
        o_ref[...] = xp_ref[...] + a

    @pl.when(i >= T_P // 512)
    def _():
        o_ref[...] = xs_ref[...] + a


def _oproj(mix, w_o_bf, xp, xs):
    tn = 1024
    np_ = T_P // 512
    return pl.pallas_call(
        _oproj_kernel,
        grid=(D_MODEL // tn, T // 512),
        in_specs=[
            pl.BlockSpec((512, D_MODEL), lambda j, i: (i, 0)),
            pl.BlockSpec((D_MODEL, tn), lambda j, i: (0, j)),
            pl.BlockSpec((512, tn), lambda j, i: (jnp.minimum(i, np_ - 1), j)),
            pl.BlockSpec((512, tn), lambda j, i: (jnp.maximum(i - np_, 0), j)),
        ],
        out_specs=pl.BlockSpec((512, tn), lambda j, i: (i, j)),
        out_shape=jax.ShapeDtypeStruct((T, D_MODEL), jnp.float32),
        compiler_params=_cparams(("arbitrary", "arbitrary")),
        name="mix_oproj",
    )(mix, w_o_bf, xp, xs)


def _f32_bits(x):
    return pltpu.bitcast(x, jnp.uint32)


def _route_kernel(h_ref, g_ref, wr_ref, br_ref, hn_ref, idx_ref, gate_ref, rank_ref, hist_ref):
    h = h_ref[...]
    hn = h * lax.rsqrt(jnp.mean(h * h, axis=-1, keepdims=True) + EPS) * g_ref[...]
    lo = _f32_bits(hn[:, :HALF].astype(jnp.bfloat16).astype(jnp.float32))
    hi = _f32_bits(hn[:, HALF:].astype(jnp.bfloat16).astype(jnp.float32))
    hn_ref[...] = jnp.right_shift(lo, jnp.uint32(16)) | (hi & jnp.uint32(0xFFFF0000))
    logits = lax.dot_general(wr_ref[...], hn, (((1,), (1,)), ((), ())),
                             precision=lax.Precision.HIGHEST,
                             preferred_element_type=jnp.float32) + br_ref[...]
    eid = lax.broadcasted_iota(jnp.int32, logits.shape, 0)
    vals, ids = [], []
    for _ in range(TOP_K):
        m = jnp.max(logits, axis=0, keepdims=True)
        sel = jnp.min(jnp.where(logits == m, eid, N_EXPERTS), axis=0, keepdims=True)
        vals.append(m)
        ids.append(sel)
        logits = jnp.where(eid == sel, -jnp.inf, logits)
    ex = [jnp.exp(v - vals[0]) for v in vals]
    den = ex[0] + ex[1] + ex[2] + ex[3]
    for k in range(TOP_K):
        idx_ref[k:k + 1, :] = ids[k]
        gate_ref[k:k + 1, :] = ex[k] / den
    tm = logits.shape[1]
    tri = (lax.broadcasted_iota(jnp.int32, (tm, tm), 0)
           <= lax.broadcasted_iota(jnp.int32, (tm, tm), 1)).astype(jnp.bfloat16)
    run = jnp.zeros((N_EXPERTS, 1), jnp.float32)
    for k in range(TOP_K):
        oh = (eid == ids[k]).astype(jnp.float32)
        seen = jnp.dot(oh.astype(jnp.bfloat16), tri, preferred_element_type=jnp.float32)
        rank = jnp.sum(oh * (seen - 1.0 + run), axis=0, keepdims=True)
        rank_ref[k:k + 1, :] = rank.astype(jnp.int32)
        run = run + jnp.sum(oh, axis=1, keepdims=True)
    hist_ref[0] = jnp.broadcast_to(run, (N_EXPERTS, 128)).astype(jnp.int32)


def _route(h, g, wr_t, br):
    tm = ROUTE_TILE
    return pl.pallas_call(
        _route_kernel,
        grid=(T // tm,),
        in_specs=[
            pl.BlockSpec((tm, D_MODEL), lambda i: (i, 0)),
            pl.BlockSpec((1, D_MODEL), lambda i: (0, 0)),
            pl.BlockSpec((N_EXPERTS, D_MODEL), lambda i: (0, 0)),
            pl.BlockSpec((N_EXPERTS, 1), lambda i: (0, 0)),
        ],
        out_specs=[
            pl.BlockSpec((tm, HALF), lambda i: (i, 0)),
            pl.BlockSpec((TOP_K, tm), lambda i: (0, i)),
            pl.BlockSpec((TOP_K, tm), lambda i: (0, i)),
            pl.BlockSpec((TOP_K, tm), lambda i: (0, i)),
            pl.BlockSpec((1, N_EXPERTS, 128), lambda i: (i, 0, 0)),
        ],
        out_shape=[
            jax.ShapeDtypeStruct((T, HALF), jnp.uint32),
            jax.ShapeDtypeStruct((TOP_K, T), jnp.int32),
            jax.ShapeDtypeStruct((TOP_K, T), jnp.float32),
            jax.ShapeDtypeStruct((TOP_K, T), jnp.int32),
            jax.ShapeDtypeStruct((T // tm, N_EXPERTS, 128), jnp.int32),
        ],
        compiler_params=_cparams(("arbitrary",)),
        name="ffn_route",
    )(h, g, wr_t, br)


def _unpack_rows(xw):
    lo = pltpu.bitcast(jnp.left_shift(xw, jnp.uint32(16)), jnp.float32).astype(jnp.bfloat16)
    hi = pltpu.bitcast(xw & jnp.uint32(0xFFFF0000), jnp.float32).astype(jnp.bfloat16)
    return lo, hi


def _experts_kernel(ue_ref, ub_ref, un_ref, meta_ref, tok_ref,
                    bgu_ref, bd_ref, hn_hbm, wgu_hbm, wd_hbm, y_hbm,
                    xbuf, act, wstage, wbf, ystage, gsem, wsem, ysem):
    u = pl.program_id(0)
    n_units = meta_ref[0]
    n_used_blocks = meta_ref[1]
    dot = functools.partial(jnp.dot, preferred_element_type=jnp.float32)

    def gather_copy(unit, r):
        tok = tok_ref[ub_ref[unit] * EB + r]
        return pltpu.make_async_copy(hn_hbm.at[pl.ds(tok, 1), :], xbuf.at[pl.ds(r, 1), :], gsem)

    def start_gather(unit):
        def body(r, carry):
            gather_copy(unit, r).start()
            return carry
        lax.fori_loop(0, un_ref[unit] * EB, body, 0)

    def wait_gather(unit):
        def body(r, carry):
            gather_copy(unit, r).wait()
            return carry
        lax.fori_loop(0, un_ref[unit] * EB, body, 0)

    def gate_up_copies(e, j, slot):
        c0 = pl.multiple_of(j * TW, TW)
        return (pltpu.make_async_copy(wgu_hbm.at[e, :, pl.ds(c0, TW)], wstage.at[slot, 0], wsem.at[slot]),
                pltpu.make_async_copy(wgu_hbm.at[e, :, pl.ds(D_FF + c0, TW)], wstage.at[slot, 1], wsem.at[slot]))

    def down_copies(e, j, slot):
        c0 = pl.multiple_of(j * 2 * TW, 2 * TW)
        return (pltpu.make_async_copy(wd_hbm.at[e, :, pl.ds(c0, TW)], wstage.at[slot, 0], wsem.at[slot]),
                pltpu.make_async_copy(wd_hbm.at[e, :, pl.ds(c0 + TW, TW)], wstage.at[slot, 1], wsem.at[slot]))

    def start(copies):
        for c in copies:
            c.start()

    def wait(copies):
        for c in copies:
            c.wait()

    def cast_tile(slot):
        wbf[:, :TW] = wstage[slot, 0].astype(jnp.bfloat16)
        wbf[:, TW:] = wstage[slot, 1].astype(jnp.bfloat16)

    def y_copy(block, j, ys):
        r0 = pl.multiple_of(block * EB, EB)
        c0 = pl.multiple_of(j * 2 * TW, 2 * TW)
        return pltpu.make_async_copy(ystage.at[ys], y_hbm.at[pl.ds(r0, EB), pl.ds(c0, 2 * TW)], ysem.at[ys])

    @pl.when(u == 0)
    def _prologue():
        start_gather(0)
        start(gate_up_copies(ue_ref[0], 0, 0))

    @pl.when(u < n_units)
    def _unit():
        e = ue_ref[u]
        b0 = ub_ref[u]
        nb = un_ref[u]
        has_next = u + 1 < n_units
        wait_gather(u)

        def gate_up_tile(j, carry):
            slot = j % 2
            wait(gate_up_copies(e, j, slot))

            @pl.when(j + 1 < NT1)
            def _():
                start(gate_up_copies(e, j + 1, 1 - slot))

            @pl.when(j + 1 == NT1)
            def _():
                start(down_copies(e, 0, 1 - slot))

            cast_tile(slot)
            bg = bgu_ref[0, pl.ds(j, 1), :]
            bl = bgu_ref[0, pl.ds(NT1 + j, 1), :]

            def rows(rb, c2):
                r0 = pl.multiple_of(rb * EB, EB)
                lo, hi = _unpack_rows(xbuf[pl.ds(r0, EB), :])
                hcat = dot(lo, wbf[:HALF, :]) + dot(hi, wbf[HALF:, :])
                glu = jnp.minimum(hcat[:, :TW] + bg, SWIGLU_LIMIT)
                lin = jnp.clip(hcat[:, TW:] + bl, -SWIGLU_LIMIT, SWIGLU_LIMIT)
                a = glu * jax.nn.sigmoid(SWIGLU_ALPHA * glu) * (lin + 1.0)
                act[j, pl.ds(r0, EB), :] = a.astype(act.dtype)
                return c2

            lax.fori_loop(0, nb, rows, 0)
            return carry

        lax.fori_loop(0, NT1, gate_up_tile, 0)

        @pl.when(has_next)
        def _():
            start_gather(u + 1)

        def down_tile(j, carry):
            slot = j % 2
            wait(down_copies(e, j, slot))

            @pl.when(j + 1 < NT2)
            def _():
                start(down_copies(e, j + 1, 1 - slot))

            @pl.when((j + 1 == NT2) & has_next)
            def _():
                start(gate_up_copies(ue_ref[u + 1], 0, 1 - slot))

            cast_tile(slot)
            bd = bd_ref[0, pl.ds(j, 1), :]

            def rows(rb, c2):
                r0 = pl.multiple_of(rb * EB, EB)
                ys = rb % 2

                @pl.when(rb >= 2)
                def _():
                    y_copy(b0, j, ys).wait()

                a = jnp.concatenate([act[jj, pl.ds(r0, EB), :] for jj in range(NT1)], axis=1)
                ystage[ys] = dot(a, wbf[...]) + bd
                y_copy(b0 + rb, j, ys).start()
                return c2

            lax.fori_loop(0, nb, rows, 0)

            @pl.when(nb >= 2)
            def _():
                y_copy(b0, j, nb % 2).wait()

            y_copy(b0, j, (nb + 1) % 2).wait()
            return carry

        lax.fori_loop(0, NT2, down_tile, 0)

    @pl.when(u == N_UNITS - 1)
    def _zero_unused_blocks():
        ystage[0] = jnp.zeros((EB, 2 * TW), jnp.float32)

        def body(b, carry):
            for j in range(NT2):
                y_copy(b, j, 0).start()
            for j in range(NT2):
                y_copy(b, j, 0).wait()
            return carry

        lax.fori_loop(n_used_blocks, N_BLOCKS, body, 0)


def _experts(unit_tables, row_tok, hn_packed, w_gate_up, b_gate_up, w_down, b_down):
    any_spec = pl.BlockSpec(memory_space=pl.ANY)
    return pl.pallas_call(
        _experts_kernel,
        grid_spec=pltpu.PrefetchScalarGridSpec(
            num_scalar_prefetch=5,
            grid=(N_UNITS,),
            in_specs=[
                pl.BlockSpec((1, 2 * NT1, TW), lambda u, ue, ub, un, meta, tok: (ue[u], 0, 0)),
                pl.BlockSpec((1, NT2, 2 * TW), lambda u, ue, ub, un, meta, tok: (ue[u], 0, 0)),
                any_spec, any_spec, any_spec,
            ],
            out_specs=any_spec,
            scratch_shapes=[
                pltpu.VMEM((UNIT_ROWS, HALF), jnp.uint32),
                pltpu.VMEM((NT1, UNIT_ROWS, TW), jnp.bfloat16),
                pltpu.VMEM((2, 2, D_MODEL, TW), jnp.float32),
                pltpu.VMEM((D_MODEL, 2 * TW), jnp.bfloat16),
                pltpu.VMEM((2, EB, 2 * TW), jnp.float32),
                pltpu.SemaphoreType.DMA(()),
                pltpu.SemaphoreType.DMA((2,)),
                pltpu.SemaphoreType.DMA((2,)),
            ],
        ),
        out_shape=jax.ShapeDtypeStruct((N_ROWS, D_MODEL), jnp.float32),
        compiler_params=_cparams(("arbitrary",)),
        name="ffn_experts",
    )(*unit_tables, row_tok, b_gate_up, b_down, hn_packed, w_gate_up, w_down)


def _combine_kernel(pos_ref, h_ref, gate_ref, g_ref, y_hbm, o_ref, rows, sem, *, tok0):
    i = pl.program_id(0)
    base = tok0 + i * COMB_TOK

    def row_copy(k, t):
        p = pos_ref[k * T + base + t]
        return pltpu.make_async_copy(y_hbm.at[pl.ds(p, 1), :], rows.at[k, pl.ds(t, 1), :], sem)

    def start(t, carry):
        for k in range(TOP_K):
            row_copy(k, t).start()
        return carry

    def wait(t, carry):
        for k in range(TOP_K):
            row_copy(k, t).wait()
        return carry

    lax.fori_loop(0, COMB_TOK, start, 0)
    lax.fori_loop(0, COMB_TOK, wait, 0)
    gate = gate_ref[...]
    acc = rows[0] * gate[:, 0:1]
    for k in range(1, TOP_K):
        acc = acc + rows[k] * gate[:, k:k + 1]
    x = h_ref[...] + acc
    y = x * lax.rsqrt(jnp.mean(x * x, axis=-1, keepdims=True) + EPS)
    o_ref[...] = y * g_ref[...]


def _combine(pos_flat, h, gate_tk, g_final, y_sorted, tok0, n_tok):
    blk0 = tok0 // COMB_TOK
    return pl.pallas_call(
        functools.partial(_combine_kernel, tok0=tok0),
        grid_spec=pltpu.PrefetchScalarGridSpec(
            num_scalar_prefetch=1,
            grid=(n_tok // COMB_TOK,),
            in_specs=[
                pl.BlockSpec((COMB_TOK, D_MODEL), lambda i, pos: (blk0 + i, 0)),
                pl.BlockSpec((COMB_TOK, TOP_K), lambda i, pos: (blk0 + i, 0)),
                pl.BlockSpec((1, D_MODEL), lambda i, pos: (0, 0)),
                pl.BlockSpec(memory_space=pl.ANY),
            ],
            out_specs=pl.BlockSpec((COMB_TOK, D_MODEL), lambda i, pos: (i, 0)),
            scratch_shapes=[
                pltpu.VMEM((TOP_K, COMB_TOK, D_MODEL), jnp.float32),
                pltpu.SemaphoreType.DMA(()),
            ],
        ),
        out_shape=jax.ShapeDtypeStruct((n_tok, D_MODEL), jnp.float32),
        compiler_params=_cparams(("arbitrary",)),
        name="ffn_combine",
    )(pos_flat, h, gate_tk, g_final, y_sorted)


def _routing(idx_t, rank_t, hist):
    i32 = jnp.int32
    tile_off = jnp.cumsum(hist, axis=0) - hist
    counts = jnp.sum(hist, axis=0)
    nblk = (counts + EB - 1) // EB
    blk_end = jnp.cumsum(nblk)
    blk_start = blk_end - nblk
    base_tab = tile_off + blk_start[None, :] * EB
    n_tiles = T // ROUTE_TILE
    onehot = idx_t.reshape(TOP_K, n_tiles, ROUTE_TILE, 1) == jnp.arange(N_EXPERTS, dtype=i32)
    base = jnp.sum(jnp.where(onehot, base_tab[None, :, None, :], 0), axis=-1)
    dest = (base.reshape(TOP_K, T) + rank_t).reshape(-1).astype(i32)
    tok = jnp.tile(jnp.arange(T, dtype=i32), TOP_K)
    row_tok = jnp.zeros((N_ROWS,), i32).at[dest].set(tok)

    n_unit_e = (nblk + UNIT_BLOCKS - 1) // UNIT_BLOCKS
    unit_end = jnp.cumsum(n_unit_e)
    n_units = unit_end[-1]
    u = jnp.minimum(jnp.arange(N_UNITS, dtype=i32), n_units - 1)
    e = jnp.minimum(jnp.sum(unit_end[None, :] <= u[:, None], axis=1), N_EXPERTS - 1).astype(i32)
    local = u - (unit_end[e] - n_unit_e[e])
    b0 = blk_start[e] + local * UNIT_BLOCKS
    nb = jnp.minimum(nblk[e] - local * UNIT_BLOCKS, UNIT_BLOCKS)
    meta = jnp.stack([n_units, blk_end[-1]]).astype(i32)
    return row_tok, dest, (e, b0.astype(i32), nb.astype(i32), meta)


def kernel(x_prompt, x_sample, cache_conv, cache_pool, norm_mix_g, w_in, conv_k, w_conv_out, w_pool_map,
           pool_scale, w_o, norm_ffn_g, w_router, b_router, w_gate_up, b_gate_up, w_down, b_down,
           norm_final_g):
    bf = jnp.bfloat16
    xp = x_prompt.reshape(T_P, D_MODEL)
    xs = x_sample.reshape(T_S, D_MODEL)
    cc_pad = jnp.pad(cache_conv[0], ((0, 0), (CONV_PAD - CONV_HIST, 0), (0, 0)))
    cp_pad = jnp.pad(cache_pool[0], ((0, 0), (POOL_PAD - POOL_HIST, 0), (0, 0)))

    xn = _norm(xp, xs, norm_mix_g)
    w_in_bf = w_in[0].astype(bf)
    z, pooled, cs_p, ps_p, cs_s, ps_s = _proj(xn, w_in_bf, conv_k[0], cc_pad, cp_pad)
    sg = _gates(xn, w_in_bf)
    mix = _merge(z, w_conv_out[0].astype(bf), pooled, w_pool_map[0].astype(bf), pool_scale, sg)
    h = _oproj(mix, w_o[0].astype(bf), xp, xs)

    hn_packed, idx_t, gate_t, rank_t, hist = _route(h, norm_ffn_g, w_router[0].T,
                                                    b_router[0].reshape(N_EXPERTS, 1))
    row_tok, dest, unit_tables = _routing(idx_t, rank_t, hist[:, :, 0])
    y_sorted = _experts(unit_tables, row_tok, hn_packed,
                        w_gate_up[0], b_gate_up[0].reshape(N_EXPERTS, 2 * NT1, TW),
                        w_down[0], b_down[0].reshape(N_EXPERTS, NT2, 2 * TW))

    gate_tk = gate_t.T
    g_final = norm_final_g.reshape(1, D_MODEL)
    y_p = _combine(dest, h, gate_tk, g_final, y_sorted, 0, T_P)
    y_s = _combine(dest, h, gate_tk, g_final, y_sorted, T_P, T_S)
    return (y_p.reshape(BATCH, SEQ, D_MODEL), y_s.reshape(DEC_BATCH, DEC_SEQ, D_MODEL),
            cs_p[None], ps_p[None], cs_s[None], ps_s[None])
```

```python
import functools

import jax
import jax.numpy as jnp
from jax import lax
from jax.experimental import pallas as pl
from jax.experimental.pallas import tpu as pltpu

D_MODEL = 4096
BATCH = 4
SEQ = 2048
DEC_BATCH = 16
DEC_SEQ = 64
D_CONV = D_MODEL // 2
D_POOL = D_MODEL // 2
CONV_HIST = 2
POOL_HIST = 15
POOL_GROUP_DIM = D_POOL // 4
N_EXPERTS = 32
TOP_K = 4
D_FF = D_MODEL
SWIGLU_ALPHA = 1.702
SWIGLU_LIMIT = 7.0
EPS = 1e-5

T_P = BATCH * SEQ
T_S = DEC_BATCH * DEC_SEQ
T = T_P + T_S
HALF = D_MODEL // 2

ROW_TILE = 1024
N_ROW_TILES = T // ROW_TILE
N_PROMPT_TILES = T_P // ROW_TILE
TC = 256
CONV_PAD = 8
POOL_PAD = 16

EB = 256
N_SLOTS = T * TOP_K
N_BLOCKS = N_SLOTS // EB + N_EXPERTS
N_ROWS = N_BLOCKS * EB
UNIT_BLOCKS = 6
UNIT_ROWS = UNIT_BLOCKS * EB
N_UNITS = N_EXPERTS + N_BLOCKS // UNIT_BLOCKS
TW = 256
NT1 = D_FF // TW
NT2 = D_MODEL // (2 * TW)
GATHER_UNROLL = 4
ROUTE_TILE = 512
COMB_TOK = 64

VMEM_LIMIT = 56 * 1024 * 1024


def _cparams(sem):
    return pltpu.CompilerParams(dimension_semantics=sem, vmem_limit_bytes=VMEM_LIMIT)


def _norm_kernel(xp_ref, xs_ref, g_ref, o_ref):
    i = pl.program_id(0)

    def body(x):
        y = x * lax.rsqrt(jnp.mean(x * x, axis=-1, keepdims=True) + EPS)
        o_ref[...] = (y * g_ref[...]).astype(o_ref.dtype)

    @pl.when(i < T_P // 512)
    def _():
        body(xp_ref[...])

    @pl.when(i >= T_P // 512)
    def _():
        body(xs_ref[...])


def _norm(xp, xs, g):
    np_ = T_P // 512
    return pl.pallas_call(
        _norm_kernel,
        grid=(T // 512,),
        in_specs=[
            pl.BlockSpec((512, D_MODEL), lambda i: (jnp.minimum(i, np_ - 1), 0)),
            pl.BlockSpec((512, D_MODEL), lambda i: (jnp.maximum(i - np_, 0), 0)),
            pl.BlockSpec((1, D_MODEL), lambda i: (0, 0)),
        ],
        out_specs=pl.BlockSpec((512, D_MODEL), lambda i: (i, 0)),
        out_shape=jax.ShapeDtypeStruct((T, D_MODEL), jnp.bfloat16),
        compiler_params=_cparams(("arbitrary",)),
        name="mix_norm",
    )(xp, xs, g)


def _conv3(ev, ck):
    return ck[2:3, :] * ev + ck[1:2, :] * pltpu.roll(ev, 1, 0) + ck[0:1, :] * pltpu.roll(ev, 2, 0)


def _window_sum(eu, group):
    s2 = eu + pltpu.roll(eu, 1, 0)
    s4 = s2 + pltpu.roll(s2, 2, 0)
    s8 = s4 + pltpu.roll(s4, 4, 0)
    s16 = s8 + pltpu.roll(s8, 8, 0)
    return jnp.where(group == 0, s2, jnp.where(group == 1, s4, jnp.where(group == 2, s8, s16)))


def _proj_kernel(xn_ref, wb_ref, wc_ref, wx_ref, wu_ref, ck_ref, cc_ref, cp_ref,
                 z_ref, pooled_ref, csp_ref, psp_ref, css_ref, pss_ref,
                 vcarry, ucarry, vs, us):
    c = pl.program_id(0)
    i = pl.program_id(1)
    xn = xn_ref[...]
    dot = functools.partial(jnp.dot, preferred_element_type=jnp.float32)
    gate_b = dot(xn, wb_ref[...])
    v = dot(xn, wc_ref[...]) * dot(xn, wx_ref[...])
    u = dot(xn, wu_ref[...])
    ck = ck_ref[...]
    group = c // (POOL_GROUP_DIM // TC)
    window = jnp.left_shift(2, group)

    @pl.when(i < N_PROMPT_TILES)
    def _prompt():
        first = (i % 2) == 0
        vh = jnp.where(first, 0.0, vcarry[...])
        uh = jnp.where(first, 0.0, ucarry[...])
        conv = _conv3(jnp.concatenate([vh, v], axis=0), ck)[CONV_PAD:]
        z_ref[...] = (gate_b * conv).astype(z_ref.dtype)
        win = _window_sum(jnp.concatenate([uh, u], axis=0), group)[POOL_PAD:]
        pos = (i % 2) * ROW_TILE + lax.broadcasted_iota(jnp.int32, (ROW_TILE, 1), 0)
        cnt = jnp.minimum(pos + 1, window).astype(jnp.float32)
        pooled_ref[...] = (win * (1.0 / cnt) - u).astype(pooled_ref.dtype)
        vcarry[...] = v[ROW_TILE - CONV_PAD:]
        ucarry[...] = u[ROW_TILE - POOL_PAD:]
        csp_ref[0] = vcarry[CONV_PAD - CONV_HIST:, :]
        psp_ref[0] = ucarry[POOL_PAD - POOL_HIST:, :]

    @pl.when(i >= N_PROMPT_TILES)
    def _sample():
        ev, eu = [], []
        for s in range(DEC_BATCH):
            ev += [cc_ref[s], v[s * DEC_SEQ:(s + 1) * DEC_SEQ]]
            eu += [cp_ref[s], u[s * DEC_SEQ:(s + 1) * DEC_SEQ]]
        conv_e = _conv3(jnp.concatenate(ev, axis=0), ck)
        win_e = _window_sum(jnp.concatenate(eu, axis=0), group)
        lv, lu = CONV_PAD + DEC_SEQ, POOL_PAD + DEC_SEQ
        conv = jnp.concatenate([conv_e[s * lv + CONV_PAD:(s + 1) * lv] for s in range(DEC_BATCH)], axis=0)
        win = jnp.concatenate([win_e[s * lu + POOL_PAD:(s + 1) * lu] for s in range(DEC_BATCH)], axis=0)
        z_ref[...] = (gate_b * conv).astype(z_ref.dtype)
        inv = 1.0 / window.astype(jnp.float32)
        pooled_ref[...] = (win * inv - u).astype(pooled_ref.dtype)
        vs[...] = v
        us[...] = u
        for s in range(DEC_BATCH):
            end = (s + 1) * DEC_SEQ
            css_ref[s] = vs[end - CONV_HIST:end, :]
            pss_ref[s] = us[end - POOL_HIST:end, :]


def _proj(xn, w_in_bf, conv_k, cache_conv_pad, cache_pool_pad):
    nsec = D_CONV // TC
    pidx = lambda c, i: (jnp.minimum(i, N_PROMPT_TILES - 1) // 2, 0, c)
    return pl.pallas_call(
        _proj_kernel,
        grid=(nsec, N_ROW_TILES),
        in_specs=[
            pl.BlockSpec((ROW_TILE, D_MODEL), lambda c, i: (i, 0)),
            pl.BlockSpec((D_MODEL, TC), lambda c, i: (0, c)),
            pl.BlockSpec((D_MODEL, TC), lambda c, i: (0, nsec + c)),
            pl.BlockSpec((D_MODEL, TC), lambda c, i: (0, 2 * nsec + c)),
            pl.BlockSpec((D_MODEL, TC), lambda c, i: (0, 3 * nsec + c)),
            pl.BlockSpec((3, TC), lambda c, i: (0, c)),
            pl.BlockSpec((DEC_BATCH, CONV_PAD, TC), lambda c, i: (0, 0, c)),
            pl.BlockSpec((DEC_BATCH, POOL_PAD, TC), lambda c, i: (0, 0, c)),
        ],
        out_specs=[
            pl.BlockSpec((ROW_TILE, TC), lambda c, i: (i, c)),
            pl.BlockSpec((ROW_TILE, TC), lambda c, i: (i, c)),
            pl.BlockSpec((1, CONV_HIST, TC), pidx),
            pl.BlockSpec((1, POOL_HIST, TC), pidx),
            pl.BlockSpec((DEC_BATCH, CONV_HIST, TC), lambda c, i: (0, 0, c)),
            pl.BlockSpec((DEC_BATCH, POOL_HIST, TC), lambda c, i: (0, 0, c)),
        ],
        out_shape=[
            jax.ShapeDtypeStruct((T, D_CONV), jnp.bfloat16),
            jax.ShapeDtypeStruct((T, D_POOL), jnp.bfloat16),
            jax.ShapeDtypeStruct((BATCH, CONV_HIST, D_CONV), jnp.float32),
            jax.ShapeDtypeStruct((BATCH, POOL_HIST, D_POOL), jnp.float32),
            jax.ShapeDtypeStruct((DEC_BATCH, CONV_HIST, D_CONV), jnp.float32),
            jax.ShapeDtypeStruct((DEC_BATCH, POOL_HIST, D_POOL), jnp.float32),
        ],
        scratch_shapes=[
            pltpu.VMEM((CONV_PAD, TC), jnp.float32),
            pltpu.VMEM((POOL_PAD, TC), jnp.float32),
            pltpu.VMEM((ROW_TILE, TC), jnp.float32),
            pltpu.VMEM((ROW_TILE, TC), jnp.float32),
        ],
        compiler_params=_cparams(("arbitrary", "arbitrary")),
        name="mix_proj",
    )(xn, w_in_bf, w_in_bf, w_in_bf, w_in_bf, conv_k, cache_conv_pad, cache_pool_pad)


def _gates_kernel(xn_ref, w_ref, o_ref):
    g = jnp.dot(xn_ref[...], w_ref[...], preferred_element_type=jnp.float32)
    o_ref[...] = jax.nn.sigmoid(g).astype(o_ref.dtype)


def _gates(xn, w_in_bf):
    tn = 1024
    off = (3 * D_CONV + D_POOL) // tn
    return pl.pallas_call(
        _gates_kernel,
        grid=(2 * D_MODEL // tn, N_ROW_TILES),
        in_specs=[
            pl.BlockSpec((ROW_TILE, D_MODEL), lambda j, i: (i, 0)),
            pl.BlockSpec((D_MODEL, tn), lambda j, i: (0, off + j)),
        ],
        out_specs=pl.BlockSpec((ROW_TILE, tn), lambda j, i: (i, j)),
        out_shape=jax.ShapeDtypeStruct((T, 2 * D_MODEL), jnp.bfloat16),
        compiler_params=_cparams(("arbitrary", "arbitrary")),
        name="mix_gates",
    )(xn, w_in_bf)


def _merge_kernel(z_ref, wco_ref, p_ref, wpm_ref, scale_ref, ga_ref, gb_ref, o_ref):
    y_conv = jnp.dot(z_ref[...], wco_ref[...], preferred_element_type=jnp.float32)
    y_pool = jnp.dot(p_ref[...], wpm_ref[0], preferred_element_type=jnp.float32) * scale_ref[...]
    mix = ga_ref[...].astype(jnp.float32) * y_conv + gb_ref[...].astype(jnp.float32) * y_pool
    o_ref[...] = mix.astype(o_ref.dtype)


def _merge(z, wco_bf, pooled, wpm_bf, pool_scale, sg):
    tn = D_MODEL // 4
    return pl.pallas_call(
        _merge_kernel,
        grid=(4, N_ROW_TILES),
        in_specs=[
            pl.BlockSpec((ROW_TILE, D_CONV), lambda j, i: (i, 0)),
            pl.BlockSpec((D_CONV, tn), lambda j, i: (0, j)),
            pl.BlockSpec((ROW_TILE, POOL_GROUP_DIM), lambda j, i: (i, j)),
            pl.BlockSpec((1, POOL_GROUP_DIM, tn), lambda j, i: (j, 0, 0)),
            pl.BlockSpec((1, tn), lambda j, i: (0, j)),
            pl.BlockSpec((ROW_TILE, tn), lambda j, i: (i, j)),
            pl.BlockSpec((ROW_TILE, tn), lambda j, i: (i, 4 + j)),
        ],
        out_specs=pl.BlockSpec((ROW_TILE, tn), lambda j, i: (i, j)),
        out_shape=jax.ShapeDtypeStruct((T, D_MODEL), jnp.bfloat16),
        compiler_params=_cparams(("arbitrary", "arbitrary")),
        name="mix_merge",
    )(z, wco_bf, pooled, wpm_bf, pool_scale, sg, sg)


def _oproj_kernel(m_ref, w_ref, xp_ref, xs_ref, o_ref):
    i = pl.program_id(1)
    a = jnp.dot(m_ref[...], w_ref[...], preferred_element_type=jnp.float32)

    @pl.when(i < T_P // 512)
    def _():
        o_ref[...] = xp_ref[...] + a

    @pl.when(i >= T_P // 512)
    def _():
        o_ref[...] = xs_ref[...] + a


def _oproj(mix, w_o_bf, xp, xs):
    tn = 1024
    np_ = T_P // 512
    return pl.pallas_call(
        _oproj_kernel,
        grid=(D_MODEL // tn, T // 512),
        in_specs=[
            pl.BlockSpec((512, D_MODEL), lambda j, i: (i, 0)),
            pl.BlockSpec((D_MODEL, tn), lambda j, i: (0, j)),
            pl.BlockSpec((512, tn), lambda j, i: (jnp.minimum(i, np_ - 1), j)),
            pl.BlockSpec((512, tn), lambda j, i: (jnp.maximum(i - np_, 0), j)),
        ],
        out_specs=pl.BlockSpec((512, tn), lambda j, i: (i, j)),
        out_shape=jax.ShapeDtypeStruct((T, D_MODEL), jnp.float32),
        compiler_params=_cparams(("arbitrary", "arbitrary")),
        name="mix_oproj",
    )(mix, w_o_bf, xp, xs)


def _f32_bits(x):
    return pltpu.bitcast(x, jnp.uint32)


def _route_kernel(h_ref, g_ref, wr_ref, br_ref, hn_ref, idx_ref, gate_ref, rank_ref, hist_ref):
    h = h_ref[...]
    hn = h * lax.rsqrt(jnp.mean(h * h, axis=-1, keepdims=True) + EPS) * g_ref[...]
    lo = _f32_bits(hn[:, :HALF].astype(jnp.bfloat16).astype(jnp.float32))
    hi = _f32_bits(hn[:, HALF:].astype(jnp.bfloat16).astype(jnp.float32))
    hn_ref[...] = jnp.right_shift(lo, jnp.uint32(16)) | (hi & jnp.uint32(0xFFFF0000))
    logits = lax.dot_general(wr_ref[...], hn, (((1,), (1,)), ((), ())),
                             precision=lax.Precision.HIGHEST,
                             preferred_element_type=jnp.float32) + br_ref[...]
    eid = lax.broadcasted_iota(jnp.int32, logits.shape, 0)
    vals, ids = [], []
    for _ in range(TOP_K):
        m = jnp.max(logits, axis=0, keepdims=True)
        sel = jnp.min(jnp.where(logits == m, eid, N_EXPERTS), axis=0, keepdims=True)
        vals.append(m)
        ids.append(sel)
        logits = jnp.where(eid == sel, -jnp.inf, logits)
    ex = [jnp.exp(v - vals[0]) for v in vals]
    den = ex[0] + ex[1] + ex[2] + ex[3]
    for k in range(TOP_K):
        idx_ref[k:k + 1, :] = ids[k]
        gate_ref[k:k + 1, :] = ex[k] / den
    tm = logits.shape[1]
    tri = (lax.broadcasted_iota(jnp.int32, (tm, tm), 0)
           <= lax.broadcasted_iota(jnp.int32, (tm, tm), 1)).astype(jnp.bfloat16)
    run = jnp.zeros((N_EXPERTS, 1), jnp.float32)
    for k in range(TOP_K):
        oh = (eid == ids[k]).astype(jnp.float32)
        seen = jnp.dot(oh.astype(jnp.bfloat16), tri, preferred_element_type=jnp.float32)
        rank = jnp.sum(oh * (seen - 1.0 + run), axis=0, keepdims=True)
        rank_ref[k:k + 1, :] = rank.astype(jnp.int32)
        run = run + jnp.sum(oh, axis=1, keepdims=True)
    hist_ref[0] = jnp.broadcast_to(run, (N_EXPERTS, 128)).astype(jnp.int32)


def _route(h, g, wr_t, br):
    tm = ROUTE_TILE
    return pl.pallas_call(
        _route_kernel,
        grid=(T // tm,),
        in_specs=[
            pl.BlockSpec((tm, D_MODEL), lambda i: (i, 0)),
            pl.BlockSpec((1, D_MODEL), lambda i: (0, 0)),
            pl.BlockSpec((N_EXPERTS, D_MODEL), lambda i: (0, 0)),
            pl.BlockSpec((N_EXPERTS, 1), lambda i: (0, 0)),
        ],
        out_specs=[
            pl.BlockSpec((tm, HALF), lambda i: (i, 0)),
            pl.BlockSpec((TOP_K, tm), lambda i: (0, i)),
            pl.BlockSpec((TOP_K, tm), lambda i: (0, i)),
            pl.BlockSpec((TOP_K, tm), lambda i: (0, i)),
            pl.BlockSpec((1, N_EXPERTS, 128), lambda i: (i, 0, 0)),
        ],
        out_shape=[
            jax.ShapeDtypeStruct((T, HALF), jnp.uint32),
            jax.ShapeDtypeStruct((TOP_K, T), jnp.int32),
            jax.ShapeDtypeStruct((TOP_K, T), jnp.float32),
            jax.ShapeDtypeStruct((TOP_K, T), jnp.int32),
            jax.ShapeDtypeStruct((T // tm, N_EXPERTS, 128), jnp.int32),
        ],
        compiler_params=_cparams(("arbitrary",)),
        name="ffn_route",
    )(h, g, wr_t, br)


def _unpack_rows(xw):
    lo = pltpu.bitcast(jnp.left_shift(xw, jnp.uint32(16)), jnp.float32).astype(jnp.bfloat16)
    hi = pltpu.bitcast(xw & jnp.uint32(0xFFFF0000), jnp.float32).astype(jnp.bfloat16)
    return lo, hi


def _experts_kernel(ue_ref, ub_ref, un_ref, meta_ref, tok_ref,
                    bgu_ref, bd_ref, hn_hbm, wgu_hbm, wd_hbm, y_hbm,
                    xbuf, act, wstage, wbf, ystage1, ystage2, gsem, wsem, y1sem, y2sem):
    u = pl.program_id(0)
    n_units = meta_ref[0]
    n_used_blocks = meta_ref[1]
    dot = functools.partial(jnp.dot, preferred_element_type=jnp.float32)

    def gather_copy(unit, r):
        tok = tok_ref[ub_ref[unit] * EB + r]
        return pltpu.make_async_copy(hn_hbm.at[pl.ds(tok, 1), :], xbuf.at[pl.ds(r, 1), :], gsem)

    def start_gather(unit):
        def body(q, carry):
            for d in range(GATHER_UNROLL):
                gather_copy(unit, q * GATHER_UNROLL + d).start()
            return carry
        lax.fori_loop(0, un_ref[unit] * (EB // GATHER_UNROLL), body, 0)

    def wait_gather(unit):
        def body(b, carry):
            r0 = pl.multiple_of(b * EB, EB)
            pltpu.make_async_copy(hn_hbm.at[pl.ds(0, EB), :], xbuf.at[pl.ds(r0, EB), :], gsem).wait()
            return carry
        lax.fori_loop(0, un_ref[unit], body, 0)

    def tile_copies(w_hbm, e, cols, slot):
        out = []
        for part, c0 in enumerate(cols):
            for kh in range(2):
                rows = pl.ds(kh * HALF, HALF)
                out.append(pltpu.make_async_copy(w_hbm.at[e, rows, pl.ds(c0, TW)],
                                                 wstage.at[slot, part, rows, :], wsem.at[slot]))
        return out

    def gate_up_copies(e, j, slot):
        c0 = pl.multiple_of(j * TW, TW)
        return tile_copies(wgu_hbm, e, (c0, D_FF + c0), slot)

    def down_copies(e, j, slot):
        c0 = pl.multiple_of(j * 2 * TW, 2 * TW)
        return tile_copies(wd_hbm, e, (c0, c0 + TW), slot)

    def start(copies):
        for c in copies:
            c.start()

    def wait(copies):
        for c in copies:
            c.wait()

    def cast_tile(slot):
        wbf[:, :TW] = wstage[slot, 0].astype(jnp.bfloat16)
        wbf[:, TW:] = wstage[slot, 1].astype(jnp.bfloat16)

    def y1_copy(block, j):
        r0 = pl.multiple_of(block * EB, EB)
        c0 = pl.multiple_of(j * 2 * TW, 2 * TW)
        return pltpu.make_async_copy(ystage1, y_hbm.at[pl.ds(r0, EB), pl.ds(c0, 2 * TW)], y1sem)

    def y2_copy(block, j, ys):
        r0 = pl.multiple_of(block * EB, EB)
        c0 = pl.multiple_of(j * 2 * TW, 2 * TW)
        return pltpu.make_async_copy(ystage2.at[ys], y_hbm.at[pl.ds(r0, 2 * EB), pl.ds(c0, 2 * TW)],
                                     y2sem.at[ys])

    def gate_up_rows(r0, m, j, bg, bl):
        lo, hi = _unpack_rows(xbuf[pl.ds(r0, m), :])
        hcat = dot(lo, wbf[:HALF, :]) + dot(hi, wbf[HALF:, :])
        glu = jnp.minimum(hcat[:, :TW] + bg, SWIGLU_LIMIT)
        lin = jnp.clip(hcat[:, TW:] + bl, -SWIGLU_LIMIT, SWIGLU_LIMIT)
        a = glu * jax.nn.sigmoid(SWIGLU_ALPHA * glu) * (lin + 1.0)
        act[j, pl.ds(r0, m), :] = a.astype(act.dtype)

    def down_rows(r0, m, bd):
        a = jnp.concatenate([act[jj, pl.ds(r0, m), :] for jj in range(NT1)], axis=1)
        return dot(a, wbf[...]) + bd

    @pl.when(u == 0)
    def _prologue():
        start_gather(0)
        start(gate_up_copies(ue_ref[0], 0, 0))

    @pl.when(u < n_units)
    def _unit():
        e = ue_ref[u]
        b0 = ub_ref[u]
        nb = un_ref[u]
        has_next = u + 1 < n_units
        wait_gather(u)

        def gate_up_tile(j, carry):
            slot = j % 2
            wait(gate_up_copies(e, j, slot))

            @pl.when(j + 1 < NT1)
            def _():
                start(gate_up_copies(e, j + 1, 1 - slot))

            @pl.when(j + 1 == NT1)
            def _():
                start(down_copies(e, 0, 1 - slot))

            bg = bgu_ref[0, pl.ds(j, 1), :]
            bl = bgu_ref[0, pl.ds(NT1 + j, 1), :]
            cast_tile(slot)
            gate_up_rows(0, EB, j, bg, bl)

            def pair(p, c2):
                gate_up_rows(pl.multiple_of(EB + p * 2 * EB, EB), 2 * EB, j, bg, bl)
                return c2

            lax.fori_loop(0, (nb - 1) // 2, pair, 0)

            @pl.when((nb - 1) % 2 == 1)
            def _():
                gate_up_rows(pl.multiple_of((nb - 1) * EB, EB), EB, j, bg, bl)

            return carry

        lax.fori_loop(0, NT1, gate_up_tile, 0)

        @pl.when(has_next)
        def _():
            start_gather(u + 1)

        def down_tile(j, carry):
            slot = j % 2
            wait(down_copies(e, j, slot))

            @pl.when(j + 1 < NT2)
            def _():
                start(down_copies(e, j + 1, 1 - slot))

            @pl.when((j + 1 == NT2) & has_next)
            def _():
                start(gate_up_copies(ue_ref[u + 1], 0, 1 - slot))

            bd = bd_ref[0, pl.ds(j, 1), :]
            n_pairs = (nb - 1) // 2
            cast_tile(slot)
            ystage1[...] = down_rows(0, EB, bd)
            y1_copy(b0, j).start()

            def pair(p, c2):
                ys = p % 2

                @pl.when(p >= 2)
                def _():
                    y2_copy(b0, j, ys).wait()

                ystage2[ys] = down_rows(pl.multiple_of(EB + p * 2 * EB, EB), 2 * EB, bd)
                y2_copy(b0 + 1 + 2 * p, j, ys).start()
                return c2

            lax.fori_loop(0, n_pairs, pair, 0)

            @pl.when((nb - 1) % 2 == 1)
            def _():
                y1_copy(b0, j).wait()
                ystage1[...] = down_rows(pl.multiple_of((nb - 1) * EB, EB), EB, bd)
                y1_copy(b0 + nb - 1, j).start()

            y1_copy(b0, j).wait()

            @pl.when(n_pairs >= 1)
            def _():
                y2_copy(b0, j, (n_pairs + 1) % 2).wait()

            @pl.when(n_pairs >= 2)
            def _():
                y2_copy(b0, j, n_pairs % 2).wait()

            return carry

        lax.fori_loop(0, NT2, down_tile, 0)

    @pl.when(u == N_UNITS - 1)
    def _zero_unused_blocks():
        ystage1[...] = jnp.zeros((EB, 2 * TW), jnp.float32)

        def body(b, carry):
            for j in range(NT2):
                y1_copy(b, j).start()
            for j in range(NT2):
                y1_copy(b, j).wait()
            return carry

        lax.fori_loop(n_used_blocks, N_BLOCKS, body, 0)


def _experts(unit_tables, row_tok, hn_packed, w_gate_up, b_gate_up, w_down, b_down):
    any_spec = pl.BlockSpec(memory_space=pl.ANY)
    return pl.pallas_call(
        _experts_kernel,
        grid_spec=pltpu.PrefetchScalarGridSpec(
            num_scalar_prefetch=5,
            grid=(N_UNITS,),
            in_specs=[
                pl.BlockSpec((1, 2 * NT1, TW), lambda u, ue, ub, un, meta, tok: (ue[u], 0, 0)),
                pl.BlockSpec((1, NT2, 2 * TW), lambda u, ue, ub, un, meta, tok: (ue[u], 0, 0)),
                any_spec, any_spec, any_spec,
            ],
            out_specs=any_spec,
            scratch_shapes=[
                pltpu.VMEM((UNIT_ROWS, HALF), jnp.uint32),
                pltpu.VMEM((NT1, UNIT_ROWS, TW), jnp.bfloat16),
                pltpu.VMEM((2, 2, D_MODEL, TW), jnp.float32),
                pltpu.VMEM((D_MODEL, 2 * TW), jnp.bfloat16),
                pltpu.VMEM((EB, 2 * TW), jnp.float32),
                pltpu.VMEM((2, 2 * EB, 2 * TW), jnp.float32),
                pltpu.SemaphoreType.DMA(()),
                pltpu.SemaphoreType.DMA((2,)),
                pltpu.SemaphoreType.DMA(()),
                pltpu.SemaphoreType.DMA((2,)),
            ],
        ),
        out_shape=jax.ShapeDtypeStruct((N_ROWS, D_MODEL), jnp.float32),
        compiler_params=_cparams(("arbitrary",)),
        name="ffn_experts",
    )(*unit_tables, row_tok, b_gate_up, b_down, hn_packed, w_gate_up, w_down)


def _combine_kernel(pos_ref, h_ref, gate_ref, g_ref, y_hbm, o_ref, rows, sem, *, tok0):
    i = pl.program_id(0)
    slot = i % 2

    def start_rows(step, dst_slot):
        base = tok0 + step * COMB_TOK

        def body(t, carry):
            for k in range(TOP_K):
                p = pos_ref[k * T + base + t]
                pltpu.make_async_copy(y_hbm.at[pl.ds(p, 1), :], rows.at[dst_slot, k, pl.ds(t, 1), :],
                                      sem.at[dst_slot]).start()
            return carry

        lax.fori_loop(0, COMB_TOK, body, 0)

    @pl.when(i == 0)
    def _():
        start_rows(0, 0)

    @pl.when(i + 1 < pl.num_programs(0))
    def _():
        start_rows(i + 1, 1 - slot)

    for k in range(TOP_K):
        pltpu.make_async_copy(y_hbm.at[pl.ds(0, COMB_TOK), :], rows.at[slot, k], sem.at[slot]).wait()
    gate = gate_ref[...]
    acc = rows[slot, 0] * gate[:, 0:1]
    for k in range(1, TOP_K):
        acc = acc + rows[slot, k] * gate[:, k:k + 1]
    x = h_ref[...] + acc
    y = x * lax.rsqrt(jnp.mean(x * x, axis=-1, keepdims=True) + EPS)
    o_ref[...] = y * g_ref[...]


def _combine(pos_flat, h, gate_tk, g_final, y_sorted, tok0, n_tok):
    blk0 = tok0 // COMB_TOK
    return pl.pallas_call(
        functools.partial(_combine_kernel, tok0=tok0),
        grid_spec=pltpu.PrefetchScalarGridSpec(
            num_scalar_prefetch=1,
            grid=(n_tok // COMB_TOK,),
            in_specs=[
                pl.BlockSpec((COMB_TOK, D_MODEL), lambda i, pos: (blk0 + i, 0)),
                pl.BlockSpec((COMB_TOK, TOP_K), lambda i, pos: (blk0 + i, 0)),
                pl.BlockSpec((1, D_MODEL), lambda i, pos: (0, 0)),
                pl.BlockSpec(memory_space=pl.ANY),
            ],
            out_specs=pl.BlockSpec((COMB_TOK, D_MODEL), lambda i, pos: (i, 0)),
            scratch_shapes=[
                pltpu.VMEM((2, TOP_K, COMB_TOK, D_MODEL), jnp.float32),
                pltpu.SemaphoreType.DMA((2,)),
            ],
        ),
        out_shape=jax.ShapeDtypeStruct((n_tok, D_MODEL), jnp.float32),
        compiler_params=_cparams(("arbitrary",)),
        name="ffn_combine",
    )(pos_flat, h, gate_tk, g_final, y_sorted)


def _routing(idx_t, rank_t, hist):
    i32 = jnp.int32
    tile_off = jnp.cumsum(hist, axis=0) - hist
    counts = jnp.sum(hist, axis=0)
    nblk = (counts + EB - 1) // EB
    blk_end = jnp.cumsum(nblk)
    blk_start = blk_end - nblk
    base_tab = tile_off + blk_start[None, :] * EB
    n_tiles = T // ROUTE_TILE
    onehot = idx_t.reshape(TOP_K, n_tiles, ROUTE_TILE, 1) == jnp.arange(N_EXPERTS, dtype=i32)
    base = jnp.sum(jnp.where(onehot, base_tab[None, :, None, :], 0), axis=-1)
    dest = (base.reshape(TOP_K, T) + rank_t).reshape(-1).astype(i32)
    tok = jnp.tile(jnp.arange(T, dtype=i32), TOP_K)
    row_tok = jnp.zeros((N_ROWS,), i32).at[dest].set(tok)

    n_unit_e = (nblk + UNIT_BLOCKS - 1) // UNIT_BLOCKS
    unit_end = jnp.cumsum(n_unit_e)
    n_units = unit_end[-1]
    u = jnp.minimum(jnp.arange(N_UNITS, dtype=i32), n_units - 1)
    e = jnp.minimum(jnp.sum(unit_end[None, :] <= u[:, None], axis=1), N_EXPERTS - 1).astype(i32)
    local = u - (unit_end[e] - n_unit_e[e])
    b0 = blk_start[e] + local * UNIT_BLOCKS
    nb = jnp.minimum(nblk[e] - local * UNIT_BLOCKS, UNIT_BLOCKS)
    meta = jnp.stack([n_units, blk_end[-1]]).astype(i32)
    return row_tok, dest, (e, b0.astype(i32), nb.astype(i32), meta)


def kernel(x_prompt, x_sample, cache_conv, cache_pool, norm_mix_g, w_in, conv_k, w_conv_out, w_pool_map,
           pool_scale, w_o, norm_ffn_g, w_router, b_router, w_gate_up, b_gate_up, w_down, b_down,
           norm_final_g):
    bf = jnp.bfloat16
    xp = x_prompt.reshape(T_P, D_MODEL)
    xs = x_sample.reshape(T_S, D_MODEL)
    cc_pad = jnp.pad(cache_conv[0], ((0, 0), (CONV_PAD - CONV_HIST, 0), (0, 0)))
    cp_pad = jnp.pad(cache_pool[0], ((0, 0), (POOL_PAD - POOL_HIST, 0), (0, 0)))

    xn = _norm(xp, xs, norm_mix_g)
    w_in_bf = w_in[0].astype(bf)
    z, pooled, cs_p, ps_p, cs_s, ps_s = _proj(xn, w_in_bf, conv_k[0], cc_pad, cp_pad)
    sg = _gates(xn, w_in_bf)
    mix = _merge(z, w_conv_out[0].astype(bf), pooled, w_pool_map[0].astype(bf), pool_scale, sg)
    h = _oproj(mix, w_o[0].astype(bf), xp, xs)

    hn_packed, idx_t, gate_t, rank_t, hist = _route(h, norm_ffn_g, w_router[0].T,
                                                    b_router[0].reshape(N_EXPERTS, 1))
    row_tok, dest, unit_tables = _routing(idx_t, rank_t, hist[:, :, 0])
    y_sorted = _experts(unit_tables, row_tok, hn_packed,
                        w_gate_up[0], b_gate_up[0].reshape(N_EXPERTS, 2 * NT1, TW),
                        w_down[0], b_down[0].reshape(N_EXPERTS, NT2, 2 * TW))

    gate_tk = gate_t.T
    g_final = norm_final_g.reshape(1, D_MODEL)
    y_p = _combine(dest, h, gate_tk, g_final, y_sorted, 0, T_P)
    y_s = _combine(dest, h, gate_tk, g_final, y_sorted, T_P, T_S)
    return (y_p.reshape(BATCH, SEQ, D_MODEL), y_s.reshape(DEC_BATCH, DEC_SEQ, D_MODEL),
            cs_p[None], ps_p[None], cs_s[None], ps_s[None])
```

```python
import functools

import jax
import jax.numpy as jnp
from jax import lax
from jax.experimental import pallas as pl
from jax.experimental.pallas import tpu as pltpu

D_MODEL = 4096
BATCH = 4
SEQ = 2048
DEC_BATCH = 16
DEC_SEQ = 64
D_CONV = D_MODEL // 2
D_POOL = D_MODEL // 2
CONV_HIST = 2
POOL_HIST = 15
POOL_GROUP_DIM = D_POOL // 4
N_EXPERTS = 32
TOP_K = 4
D_FF = D_MODEL
SWIGLU_ALPHA = 1.702
SWIGLU_LIMIT = 7.0
EPS = 1e-5

T_P = BATCH * SEQ
T_S = DEC_BATCH * DEC_SEQ
T = T_P + T_S
HALF = D_MODEL // 2

ROW_TILE = 1024
N_ROW_TILES = T // ROW_TILE
N_PROMPT_TILES = T_P // ROW_TILE
TC = 256
CONV_PAD = 8
POOL_PAD = 16

EB = 256
N_SLOTS = T * TOP_K
N_BLOCKS = N_SLOTS // EB + N_EXPERTS
N_ROWS = N_BLOCKS * EB
UNIT_BLOCKS = 6
UNIT_ROWS = UNIT_BLOCKS * EB
N_UNITS = N_EXPERTS + N_BLOCKS // UNIT_BLOCKS
TW = 256
NT1 = D_FF // TW
NT2 = D_MODEL // (2 * TW)
GATHER_UNROLL = 4
ROUTE_TILE = 512
COMB_TOK = 64

VMEM_LIMIT = 56 * 1024 * 1024


def _cparams(sem):
    return pltpu.CompilerParams(dimension_semantics=sem, vmem_limit_bytes=VMEM_LIMIT)


def _norm_kernel(xp_ref, xs_ref, g_ref, o_ref):
    i = pl.program_id(0)

    def body(x):
        y = x * lax.rsqrt(jnp.mean(x * x, axis=-1, keepdims=True) + EPS)
        o_ref[...] = (y * g_ref[...]).astype(o_ref.dtype)

    @pl.when(i < T_P // 512)
    def _():
        body(xp_ref[...])

    @pl.when(i >= T_P // 512)
    def _():
        body(xs_ref[...])


def _norm(xp, xs, g):
    np_ = T_P // 512
    return pl.pallas_call(
        _norm_kernel,
        grid=(T // 512,),
        in_specs=[
            pl.BlockSpec((512, D_MODEL), lambda i: (jnp.minimum(i, np_ - 1), 0)),
            pl.BlockSpec((512, D_MODEL), lambda i: (jnp.maximum(i - np_, 0), 0)),
            pl.BlockSpec((1, D_MODEL), lambda i: (0, 0)),
        ],
        out_specs=pl.BlockSpec((512, D_MODEL), lambda i: (i, 0)),
        out_shape=jax.ShapeDtypeStruct((T, D_MODEL), jnp.bfloat16),
        compiler_params=_cparams(("arbitrary",)),
        name="mix_norm",
    )(xp, xs, g)


def _conv3(ev, ck):
    return ck[2:3, :] * ev + ck[1:2, :] * pltpu.roll(ev, 1, 0) + ck[0:1, :] * pltpu.roll(ev, 2, 0)


def _window_sum(eu, group):
    s2 = eu + pltpu.roll(eu, 1, 0)
    s4 = s2 + pltpu.roll(s2, 2, 0)
    s8 = s4 + pltpu.roll(s4, 4, 0)
    s16 = s8 + pltpu.roll(s8, 8, 0)
    return jnp.where(group == 0, s2, jnp.where(group == 1, s4, jnp.where(group == 2, s8, s16)))


def _proj_kernel(xn_ref, wb_ref, wc_ref, wx_ref, wu_ref, ck_ref, cc_ref, cp_ref,
                 z_ref, pooled_ref, csp_ref, psp_ref, css_ref, pss_ref,
                 vcarry, ucarry, vs, us):
    c = pl.program_id(0)
    i = pl.program_id(1)
    xn = xn_ref[...]
    dot = functools.partial(jnp.dot, preferred_element_type=jnp.float32)
    gate_b = dot(xn, wb_ref[...])
    v = dot(xn, wc_ref[...]) * dot(xn, wx_ref[...])
    u = dot(xn, wu_ref[...])
    ck = ck_ref[...]
    group = c // (POOL_GROUP_DIM // TC)
    window = jnp.left_shift(2, group)

    @pl.when(i < N_PROMPT_TILES)
    def _prompt():
        first = (i % 2) == 0
        vh = jnp.where(first, 0.0, vcarry[...])
        uh = jnp.where(first, 0.0, ucarry[...])
        conv = _conv3(jnp.concatenate([vh, v], axis=0), ck)[CONV_PAD:]
        z_ref[...] = (gate_b * conv).astype(z_ref.dtype)
        win = _window_sum(jnp.concatenate([uh, u], axis=0), group)[POOL_PAD:]
        pos = (i % 2) * ROW_TILE + lax.broadcasted_iota(jnp.int32, (ROW_TILE, 1), 0)
        cnt = jnp.minimum(pos + 1, window).astype(jnp.float32)
        pooled_ref[...] = (win * (1.0 / cnt) - u).astype(pooled_ref.dtype)
        vcarry[...] = v[ROW_TILE - CONV_PAD:]
        ucarry[...] = u[ROW_TILE - POOL_PAD:]
        csp_ref[0] = vcarry[CONV_PAD - CONV_HIST:, :]
        psp_ref[0] = ucarry[POOL_PAD - POOL_HIST:, :]

    @pl.when(i >= N_PROMPT_TILES)
    def _sample():
        ev, eu = [], []
        for s in range(DEC_BATCH):
            ev += [cc_ref[s], v[s * DEC_SEQ:(s + 1) * DEC_SEQ]]
            eu += [cp_ref[s], u[s * DEC_SEQ:(s + 1) * DEC_SEQ]]
        conv_e = _conv3(jnp.concatenate(ev, axis=0), ck)
        win_e = _window_sum(jnp.concatenate(eu, axis=0), group)
        lv, lu = CONV_PAD + DEC_SEQ, POOL_PAD + DEC_SEQ
        conv = jnp.concatenate([conv_e[s * lv + CONV_PAD:(s + 1) * lv] for s in range(DEC_BATCH)], axis=0)
        win = jnp.concatenate([win_e[s * lu + POOL_PAD:(s + 1) * lu] for s in range(DEC_BATCH)], axis=0)
        z_ref[...] = (gate_b * conv).astype(z_ref.dtype)
        inv = 1.0 / window.astype(jnp.float32)
        pooled_ref[...] = (win * inv - u).astype(pooled_ref.dtype)
        vs[...] = v
        us[...] = u
        for s in range(DEC_BATCH):
            end = (s + 1) * DEC_SEQ
            css_ref[s] = vs[end - CONV_HIST:end, :]
            pss_ref[s] = us[end - POOL_HIST:end, :]


def _proj(xn, w_in_bf, conv_k, cache_conv_pad, cache_pool_pad):
    nsec = D_CONV // TC
    pidx = lambda c, i: (jnp.minimum(i, N_PROMPT_TILES - 1) // 2, 0, c)
    return pl.pallas_call(
        _proj_kernel,
        grid=(nsec, N_ROW_TILES),
        in_specs=[
            pl.BlockSpec((ROW_TILE, D_MODEL), lambda c, i: (i, 0)),
            pl.BlockSpec((D_MODEL, TC), lambda c, i: (0, c)),
            pl.BlockSpec((D_MODEL, TC), lambda c, i: (0, nsec + c)),
            pl.BlockSpec((D_MODEL, TC), lambda c, i: (0, 2 * nsec + c)),
            pl.BlockSpec((D_MODEL, TC), lambda c, i: (0, 3 * nsec + c)),
            pl.BlockSpec((3, TC), lambda c, i: (0, c)),
            pl.BlockSpec((DEC_BATCH, CONV_PAD, TC), lambda c, i: (0, 0, c)),
            pl.BlockSpec((DEC_BATCH, POOL_PAD, TC), lambda c, i: (0, 0, c)),
        ],
        out_specs=[
            pl.BlockSpec((ROW_TILE, TC), lambda c, i: (i, c)),
            pl.BlockSpec((ROW_TILE, TC), lambda c, i: (i, c)),
            pl.BlockSpec((1, CONV_HIST, TC), pidx),
            pl.BlockSpec((1, POOL_HIST, TC), pidx),
            pl.BlockSpec((DEC_BATCH, CONV_HIST, TC), lambda c, i: (0, 0, c)),
            pl.BlockSpec((DEC_BATCH, POOL_HIST, TC), lambda c, i: (0, 0, c)),
        ],
        out_shape=[
            jax.ShapeDtypeStruct((T, D_CONV), jnp.bfloat16),
            jax.ShapeDtypeStruct((T, D_POOL), jnp.bfloat16),
            jax.ShapeDtypeStruct((BATCH, CONV_HIST, D_CONV), jnp.float32),
            jax.ShapeDtypeStruct((BATCH, POOL_HIST, D_POOL), jnp.float32),
            jax.ShapeDtypeStruct((DEC_BATCH, CONV_HIST, D_CONV), jnp.float32),
            jax.ShapeDtypeStruct((DEC_BATCH, POOL_HIST, D_POOL), jnp.float32),
        ],
        scratch_shapes=[
            pltpu.VMEM((CONV_PAD, TC), jnp.float32),
            pltpu.VMEM((POOL_PAD, TC), jnp.float32),
            pltpu.VMEM((ROW_TILE, TC), jnp.float32),
            pltpu.VMEM((ROW_TILE, TC), jnp.float32),
        ],
        compiler_params=_cparams(("arbitrary", "arbitrary")),
        name="mix_proj",
    )(xn, w_in_bf, w_in_bf, w_in_bf, w_in_bf, conv_k, cache_conv_pad, cache_pool_pad)


def _gates_kernel(xn_ref, w_ref, o_ref):
    g = jnp.dot(xn_ref[...], w_ref[...], preferred_element_type=jnp.float32)
    o_ref[...] = jax.nn.sigmoid(g).astype(o_ref.dtype)


def _gates(xn, w_in_bf):
    tn = 1024
    off = (3 * D_CONV + D_POOL) // tn
    return pl.pallas_call(
        _gates_kernel,
        grid=(2 * D_MODEL // tn, N_ROW_TILES),
        in_specs=[
            pl.BlockSpec((ROW_TILE, D_MODEL), lambda j, i: (i, 0)),
            pl.BlockSpec((D_MODEL, tn), lambda j, i: (0, off + j)),
        ],
        out_specs=pl.BlockSpec((ROW_TILE, tn), lambda j, i: (i, j)),
        out_shape=jax.ShapeDtypeStruct((T, 2 * D_MODEL), jnp.bfloat16),
        compiler_params=_cparams(("arbitrary", "arbitrary")),
        name="mix_gates",
    )(xn, w_in_bf)


def _merge_kernel(z_ref, wco_ref, p_ref, wpm_ref, scale_ref, ga_ref, gb_ref, o_ref):
    y_conv = jnp.dot(z_ref[...], wco_ref[...], preferred_element_type=jnp.float32)
    y_pool = jnp.dot(p_ref[...], wpm_ref[0], preferred_element_type=jnp.float32) * scale_ref[...]
    mix = ga_ref[...].astype(jnp.float32) * y_conv + gb_ref[...].astype(jnp.float32) * y_pool
    o_ref[...] = mix.astype(o_ref.dtype)


def _merge(z, wco_bf, pooled, wpm_bf, pool_scale, sg):
    tn = D_MODEL // 4
    return pl.pallas_call(
        _merge_kernel,
        grid=(4, N_ROW_TILES),
        in_specs=[
            pl.BlockSpec((ROW_TILE, D_CONV), lambda j, i: (i, 0)),
            pl.BlockSpec((D_CONV, tn), lambda j, i: (0, j)),
            pl.BlockSpec((ROW_TILE, POOL_GROUP_DIM), lambda j, i: (i, j)),
            pl.BlockSpec((1, POOL_GROUP_DIM, tn), lambda j, i: (j, 0, 0)),
            pl.BlockSpec((1, tn), lambda j, i: (0, j)),
            pl.BlockSpec((ROW_TILE, tn), lambda j, i: (i, j)),
            pl.BlockSpec((ROW_TILE, tn), lambda j, i: (i, 4 + j)),
        ],
        out_specs=pl.BlockSpec((ROW_TILE, tn), lambda j, i: (i, j)),
        out_shape=jax.ShapeDtypeStruct((T, D_MODEL), jnp.bfloat16),
        compiler_params=_cparams(("arbitrary", "arbitrary")),
        name="mix_merge",
    )(z, wco_bf, pooled, wpm_bf, pool_scale, sg, sg)


def _oproj_kernel(m_ref, w_ref, xp_ref, xs_ref, o_ref):
    i = pl.program_id(1)
    a = jnp.dot(m_ref[...], w_ref[...], preferred_element_type=jnp.float32)

    @pl.when(i < T_P // 512)
    def _():
        o_ref[...] = xp_ref[...] + a

    @pl.when(i >= T_P // 512)
    def _():
        o_ref[...] = xs_ref[...] + a


def _oproj(mix, w_o_bf, xp, xs):
    tn = 1024
    np_ = T_P // 512
    return pl.pallas_call(
        _oproj_kernel,
        grid=(D_MODEL // tn, T // 512),
        in_specs=[
            pl.BlockSpec((512, D_MODEL), lambda j, i: (i, 0)),
            pl.BlockSpec((D_MODEL, tn), lambda j, i: (0, j)),
            pl.BlockSpec((512, tn), lambda j, i: (jnp.minimum(i, np_ - 1), j)),
            pl.BlockSpec((512, tn), lambda j, i: (jnp.maximum(i - np_, 0), j)),
        ],
        out_specs=pl.BlockSpec((512, tn), lambda j, i: (i, j)),
        out_shape=jax.ShapeDtypeStruct((T, D_MODEL), jnp.float32),
        compiler_params=_cparams(("arbitrary", "arbitrary")),
        name="mix_oproj",
    )(mix, w_o_bf, xp, xs)


def _f32_bits(x):
    return pltpu.bitcast(x, jnp.uint32)


def _route_kernel(h_ref, g_ref, wr_ref, br_ref, hn_ref, idx_ref, gate_ref, rank_ref, hist_ref):
    h = h_ref[...]
    hn = h * lax.rsqrt(jnp.mean(h * h, axis=-1, keepdims=True) + EPS) * g_ref[...]
    lo = _f32_bits(hn[:, :HALF].astype(jnp.bfloat16).astype(jnp.float32))
    hi = _f32_bits(hn[:, HALF:].astype(jnp.bfloat16).astype(jnp.float32))
    hn_ref[...] = jnp.right_shift(lo, jnp.uint32(16)) | (hi & jnp.uint32(0xFFFF0000))
    logits = lax.dot_general(wr_ref[...], hn, (((1,), (1,)), ((), ())),
                             precision=lax.Precision.HIGHEST,
                             preferred_element_type=jnp.float32) + br_ref[...]
    eid = lax.broadcasted_iota(jnp.int32, logits.shape, 0)
    vals, ids = [], []
    for _ in range(TOP_K):
        m = jnp.max(logits, axis=0, keepdims=True)
        sel = jnp.min(jnp.where(logits == m, eid, N_EXPERTS), axis=0, keepdims=True)
        vals.append(m)
        ids.append(sel)
        logits = jnp.where(eid == sel, -jnp.inf, logits)
    ex = [jnp.exp(v - vals[0]) for v in vals]
    den = ex[0] + ex[1] + ex[2] + ex[3]
    for k in range(TOP_K):
        idx_ref[k:k + 1, :] = ids[k]
        gate_ref[k:k + 1, :] = ex[k] / den
    tm = logits.shape[1]
    tri = (lax.broadcasted_iota(jnp.int32, (tm, tm), 0)
           <= lax.broadcasted_iota(jnp.int32, (tm, tm), 1)).astype(jnp.bfloat16)
    run = jnp.zeros((N_EXPERTS, 1), jnp.float32)
    for k in range(TOP_K):
        oh = (eid == ids[k]).astype(jnp.float32)
        seen = jnp.dot(oh.astype(jnp.bfloat16), tri, preferred_element_type=jnp.float32)
        rank = jnp.sum(oh * (seen - 1.0 + run), axis=0, keepdims=True)
        rank_ref[k:k + 1, :] = rank.astype(jnp.int32)
        run = run + jnp.sum(oh, axis=1, keepdims=True)
    hist_ref[0] = jnp.broadcast_to(run, (N_EXPERTS, 128)).astype(jnp.int32)


def _route(h, g, wr_t, br):
    tm = ROUTE_TILE
    return pl.pallas_call(
        _route_kernel,
        grid=(T // tm,),
        in_specs=[
            pl.BlockSpec((tm, D_MODEL), lambda i: (i, 0)),
            pl.BlockSpec((1, D_MODEL), lambda i: (0, 0)),
            pl.BlockSpec((N_EXPERTS, D_MODEL), lambda i: (0, 0)),
            pl.BlockSpec((N_EXPERTS, 1), lambda i: (0, 0)),
        ],
        out_specs=[
            pl.BlockSpec((tm, HALF), lambda i: (i, 0)),
            pl.BlockSpec((TOP_K, tm), lambda i: (0, i)),
            pl.BlockSpec((TOP_K, tm), lambda i: (0, i)),
            pl.BlockSpec((TOP_K, tm), lambda i: (0, i)),
            pl.BlockSpec((1, N_EXPERTS, 128), lambda i: (i, 0, 0)),
        ],
        out_shape=[
            jax.ShapeDtypeStruct((T, HALF), jnp.uint32),
            jax.ShapeDtypeStruct((TOP_K, T), jnp.int32),
            jax.ShapeDtypeStruct((TOP_K, T), jnp.float32),
            jax.ShapeDtypeStruct((TOP_K, T), jnp.int32),
            jax.ShapeDtypeStruct((T // tm, N_EXPERTS, 128), jnp.int32),
        ],
        compiler_params=_cparams(("arbitrary",)),
        name="ffn_route",
    )(h, g, wr_t, br)


def _unpack_rows(xw):
    lo = pltpu.bitcast(jnp.left_shift(xw, jnp.uint32(16)), jnp.float32).astype(jnp.bfloat16)
    hi = pltpu.bitcast(xw & jnp.uint32(0xFFFF0000), jnp.float32).astype(jnp.bfloat16)
    return lo, hi


def _experts_kernel(ue_ref, ub_ref, un_ref, meta_ref, tok_ref,
                    bgu_ref, bd_ref, hn_hbm, wgu_hbm, wd_hbm, y_hbm,
                    xbuf, act, wstage, wbf, ystage, gsem, wsem, ysem):
    u = pl.program_id(0)
    n_units = meta_ref[0]
    n_used_blocks = meta_ref[1]
    dot = functools.partial(jnp.dot, preferred_element_type=jnp.float32)

    def gather_copy(unit, r):
        tok = tok_ref[ub_ref[unit] * EB + r]
        return pltpu.make_async_copy(hn_hbm.at[pl.ds(tok, 1), :], xbuf.at[pl.ds(r, 1), :], gsem)

    def start_gather(unit):
        def body(q, carry):
            for d in range(GATHER_UNROLL):
                gather_copy(unit, q * GATHER_UNROLL + d).start()
            return carry
        lax.fori_loop(0, un_ref[unit] * (EB // GATHER_UNROLL), body, 0)

    def wait_gather(unit):
        def body(b, carry):
            r0 = pl.multiple_of(b * EB, EB)
            pltpu.make_async_copy(hn_hbm.at[pl.ds(0, EB), :], xbuf.at[pl.ds(r0, EB), :], gsem).wait()
            return carry
        lax.fori_loop(0, un_ref[unit], body, 0)

    def tile_copies(w_hbm, e, cols, slot):
        out = []
        for part, c0 in enumerate(cols):
            for kh in range(2):
                rows = pl.ds(kh * HALF, HALF)
                out.append(pltpu.make_async_copy(w_hbm.at[e, rows, pl.ds(c0, TW)],
                                                 wstage.at[slot, part, rows, :], wsem.at[slot]))
        return out

    def gate_up_copies(e, j, slot):
        c0 = pl.multiple_of(j * TW, TW)
        return tile_copies(wgu_hbm, e, (c0, D_FF + c0), slot)

    def down_copies(e, j, slot):
        c0 = pl.multiple_of(j * 2 * TW, 2 * TW)
        return tile_copies(wd_hbm, e, (c0, c0 + TW), slot)

    def start(copies):
        for c in copies:
            c.start()

    def wait_tile(slot):
        pltpu.make_async_copy(wgu_hbm.at[pl.ds(0, 2), :, pl.ds(0, TW)], wstage.at[slot], wsem.at[slot]).wait()

    def cast_tile(slot):
        wbf[:, :TW] = wstage[slot, 0].astype(jnp.bfloat16)
        wbf[:, TW:] = wstage[slot, 1].astype(jnp.bfloat16)

    def y_copy(block, j, ys):
        r0 = pl.multiple_of(block * EB, EB)
        c0 = pl.multiple_of(j * 2 * TW, 2 * TW)
        return pltpu.make_async_copy(ystage.at[ys], y_hbm.at[pl.ds(r0, UNIT_ROWS), pl.ds(c0, 2 * TW)],
                                     ysem.at[ys])

    def zero_copy(block, j):
        r0 = pl.multiple_of(block * EB, EB)
        c0 = pl.multiple_of(j * 2 * TW, 2 * TW)
        return pltpu.make_async_copy(ystage.at[0, pl.ds(0, EB), :],
                                     y_hbm.at[pl.ds(r0, EB), pl.ds(c0, 2 * TW)], ysem.at[0])

    def gate_up_rows(r0, j, bg, bl):
        lo, hi = _unpack_rows(xbuf[pl.ds(r0, EB), :])
        hcat = dot(lo, wbf[:HALF, :]) + dot(hi, wbf[HALF:, :])
        glu = jnp.minimum(hcat[:, :TW] + bg, SWIGLU_LIMIT)
        lin = jnp.clip(hcat[:, TW:] + bl, -SWIGLU_LIMIT, SWIGLU_LIMIT)
        a = glu * jax.nn.sigmoid(SWIGLU_ALPHA * glu) * (lin + 1.0)
        act[j, pl.ds(r0, EB), :] = a.astype(act.dtype)

    def down_rows(r0, ys, bd):
        a = jnp.concatenate([act[jj, pl.ds(r0, EB), :] for jj in range(NT1)], axis=1)
        ystage[ys, pl.ds(r0, EB), :] = dot(a, wbf[...]) + bd

    def for_blocks_after_first(nb, block_fn):
        rest = nb - 1
        for run in (4, 2, 1):
            @pl.when((rest & run) != 0)
            def _(run=run):
                first = 1 + (rest & ~(2 * run - 1))
                for q in range(run):
                    block_fn(pl.multiple_of((first + q) * EB, EB))

    @pl.when(u == 0)
    def _prologue():
        ystage[...] = jnp.zeros(ystage.shape, ystage.dtype)
        start_gather(0)
        start(gate_up_copies(ue_ref[0], 0, 0))

    @pl.when(u < n_units)
    def _unit():
        e = ue_ref[u]
        b0 = ub_ref[u]
        nb = un_ref[u]
        has_next = u + 1 < n_units
        wait_gather(u)

        def gate_up_tile(j, carry):
            slot = j % 2
            wait_tile(slot)

            @pl.when(j + 1 < NT1)
            def _():
                start(gate_up_copies(e, j + 1, 1 - slot))

            @pl.when(j + 1 == NT1)
            def _():
                start(down_copies(e, 0, 1 - slot))

            bg = bgu_ref[0, pl.ds(j, 1), :]
            bl = bgu_ref[0, pl.ds(NT1 + j, 1), :]
            cast_tile(slot)
            gate_up_rows(0, j, bg, bl)
            for_blocks_after_first(nb, lambda r0: gate_up_rows(r0, j, bg, bl))
            return carry

        lax.fori_loop(0, NT1, gate_up_tile, 0)

        @pl.when(has_next)
        def _():
            start_gather(u + 1)

        def down_tile(j, carry):
            slot = j % 2
            wait_tile(slot)

            @pl.when(j + 1 < NT2)
            def _():
                start(down_copies(e, j + 1, 1 - slot))

            @pl.when((j + 1 == NT2) & has_next)
            def _():
                start(gate_up_copies(ue_ref[u + 1], 0, 1 - slot))

            bd = bd_ref[0, pl.ds(j, 1), :]
            ys = j % 2

            @pl.when(j >= 2)
            def _():
                y_copy(b0, j, ys).wait()

            cast_tile(slot)
            down_rows(0, ys, bd)
            for_blocks_after_first(nb, lambda r0: down_rows(r0, ys, bd))
            y_copy(b0, j, ys).start()
            return carry

        lax.fori_loop(0, NT2, down_tile, 0)
        y_copy(b0, 0, 0).wait()
        y_copy(b0, 0, 1).wait()

    @pl.when(u == N_UNITS - 1)
    def _zero_unused_rows():
        ystage[0] = jnp.zeros(ystage.shape[1:], ystage.dtype)

        def body(b, carry):
            for j in range(NT2):
                zero_copy(b, j).start()
            for j in range(NT2):
                zero_copy(b, j).wait()
            return carry

        lax.fori_loop(n_used_blocks, N_BLOCKS + UNIT_BLOCKS, body, 0)


def _experts(unit_tables, row_tok, hn_packed, w_gate_up, b_gate_up, w_down, b_down):
    any_spec = pl.BlockSpec(memory_space=pl.ANY)
    return pl.pallas_call(
        _experts_kernel,
        grid_spec=pltpu.PrefetchScalarGridSpec(
            num_scalar_prefetch=5,
            grid=(N_UNITS,),
            in_specs=[
                pl.BlockSpec((1, 2 * NT1, TW), lambda u, ue, ub, un, meta, tok: (ue[u], 0, 0)),
                pl.BlockSpec((1, NT2, 2 * TW), lambda u, ue, ub, un, meta, tok: (ue[u], 0, 0)),
                any_spec, any_spec, any_spec,
            ],
            out_specs=any_spec,
            scratch_shapes=[
                pltpu.VMEM((UNIT_ROWS, HALF), jnp.uint32),
                pltpu.VMEM((NT1, UNIT_ROWS, TW), jnp.bfloat16),
                pltpu.VMEM((2, 2, D_MODEL, TW), jnp.float32),
                pltpu.VMEM((D_MODEL, 2 * TW), jnp.bfloat16),
                pltpu.VMEM((2, UNIT_ROWS, 2 * TW), jnp.float32),
                pltpu.SemaphoreType.DMA(()),
                pltpu.SemaphoreType.DMA((2,)),
                pltpu.SemaphoreType.DMA((2,)),
            ],
        ),
        out_shape=jax.ShapeDtypeStruct((N_ROWS + UNIT_ROWS, D_MODEL), jnp.float32),
        compiler_params=_cparams(("arbitrary",)),
        name="ffn_experts",
    )(*unit_tables, row_tok, b_gate_up, b_down, hn_packed, w_gate_up, w_down)


def _combine_kernel(pos_ref, h_ref, gate_ref, g_ref, y_hbm, o_ref, rows, sem, *, tok0):
    i = pl.program_id(0)
    slot = i % 2

    def start_rows(step, dst_slot):
        base = tok0 + step * COMB_TOK

        def body(t, carry):
            for k in range(TOP_K):
                p = pos_ref[k * T + base + t]
                pltpu.make_async_copy(y_hbm.at[pl.ds(p, 1), :], rows.at[dst_slot, k, pl.ds(t, 1), :],
                                      sem.at[dst_slot]).start()
            return carry

        lax.fori_loop(0, COMB_TOK, body, 0)

    @pl.when(i == 0)
    def _():
        start_rows(0, 0)

    @pl.when(i + 1 < pl.num_programs(0))
    def _():
        start_rows(i + 1, 1 - slot)

    for k in range(TOP_K):
        pltpu.make_async_copy(y_hbm.at[pl.ds(0, COMB_TOK), :], rows.at[slot, k], sem.at[slot]).wait()
    gate = gate_ref[...]
    acc = rows[slot, 0] * gate[:, 0:1]
    for k in range(1, TOP_K):
        acc = acc + rows[slot, k] * gate[:, k:k + 1]
    x = h_ref[...] + acc
    y = x * lax.rsqrt(jnp.mean(x * x, axis=-1, keepdims=True) + EPS)
    o_ref[...] = y * g_ref[...]


def _combine(pos_flat, h, gate_tk, g_final, y_sorted, tok0, n_tok):
    blk0 = tok0 // COMB_TOK
    return pl.pallas_call(
        functools.partial(_combine_kernel, tok0=tok0),
        grid_spec=pltpu.PrefetchScalarGridSpec(
            num_scalar_prefetch=1,
            grid=(n_tok // COMB_TOK,),
            in_specs=[
                pl.BlockSpec((COMB_TOK, D_MODEL), lambda i, pos: (blk0 + i, 0)),
                pl.BlockSpec((COMB_TOK, TOP_K), lambda i, pos: (blk0 + i, 0)),
                pl.BlockSpec((1, D_MODEL), lambda i, pos: (0, 0)),
                pl.BlockSpec(memory_space=pl.ANY),
            ],
            out_specs=pl.BlockSpec((COMB_TOK, D_MODEL), lambda i, pos: (i, 0)),
            scratch_shapes=[
                pltpu.VMEM((2, TOP_K, COMB_TOK, D_MODEL), jnp.float32),
                pltpu.SemaphoreType.DMA((2,)),
            ],
        ),
        out_shape=jax.ShapeDtypeStruct((n_tok, D_MODEL), jnp.float32),
        compiler_params=_cparams(("arbitrary",)),
        name="ffn_combine",
    )(pos_flat, h, gate_tk, g_final, y_sorted)


def _routing(idx_t, rank_t, hist):
    i32 = jnp.int32
    tile_off = jnp.cumsum(hist, axis=0) - hist
    counts = jnp.sum(hist, axis=0)
    nblk = (counts + EB - 1) // EB
    blk_end = jnp.cumsum(nblk)
    blk_start = blk_end - nblk
    base_tab = tile_off + blk_start[None, :] * EB
    n_tiles = T // ROUTE_TILE
    onehot = idx_t.reshape(TOP_K, n_tiles, ROUTE_TILE, 1) == jnp.arange(N_EXPERTS, dtype=i32)
    base = jnp.sum(jnp.where(onehot, base_tab[None, :, None, :], 0), axis=-1)
    dest = (base.reshape(TOP_K, T) + rank_t).reshape(-1).astype(i32)
    tok = jnp.tile(jnp.arange(T, dtype=i32), TOP_K)
    row_tok = jnp.zeros((N_ROWS,), i32).at[dest].set(tok)

    n_unit_e = (nblk + UNIT_BLOCKS - 1) // UNIT_BLOCKS
    unit_end = jnp.cumsum(n_unit_e)
    n_units = unit_end[-1]
    u = jnp.minimum(jnp.arange(N_UNITS, dtype=i32), n_units - 1)
    e = jnp.minimum(jnp.sum(unit_end[None, :] <= u[:, None], axis=1), N_EXPERTS - 1).astype(i32)
    local = u - (unit_end[e] - n_unit_e[e])
    b0 = blk_start[e] + local * UNIT_BLOCKS
    nb = jnp.minimum(nblk[e] - local * UNIT_BLOCKS, UNIT_BLOCKS)
    meta = jnp.stack([n_units, blk_end[-1]]).astype(i32)
    return row_tok, dest, (e, b0.astype(i32), nb.astype(i32), meta)


def kernel(x_prompt, x_sample, cache_conv, cache_pool, norm_mix_g, w_in, conv_k, w_conv_out, w_pool_map,
           pool_scale, w_o, norm_ffn_g, w_router, b_router, w_gate_up, b_gate_up, w_down, b_down,
           norm_final_g):
    bf = jnp.bfloat16
    xp = x_prompt.reshape(T_P, D_MODEL)
    xs = x_sample.reshape(T_S, D_MODEL)
    cc_pad = jnp.pad(cache_conv[0], ((0, 0), (CONV_PAD - CONV_HIST, 0), (0, 0)))
    cp_pad = jnp.pad(cache_pool[0], ((0, 0), (POOL_PAD - POOL_HIST, 0), (0, 0)))

    xn = _norm(xp, xs, norm_mix_g)
    w_in_bf = w_in[0].astype(bf)
    z, pooled, cs_p, ps_p, cs_s, ps_s = _proj(xn, w_in_bf, conv_k[0], cc_pad, cp_pad)
    sg = _gates(xn, w_in_bf)
    mix = _merge(z, w_conv_out[0].astype(bf), pooled, w_pool_map[0].astype(bf), pool_scale, sg)
    h = _oproj(mix, w_o[0].astype(bf), xp, xs)

    hn_packed, idx_t, gate_t, rank_t, hist = _route(h, norm_ffn_g, w_router[0].T,
                                                    b_router[0].reshape(N_EXPERTS, 1))
    row_tok, dest, unit_tables = _routing(idx_t, rank_t, hist[:, :, 0])
    y_sorted = _experts(unit_tables, row_tok, hn_packed,
                        w_gate_up[0], b_gate_up[0].reshape(N_EXPERTS, 2 * NT1, TW),
                        w_down[0], b_down[0].reshape(N_EXPERTS, NT2, 2 * TW))

    gate_tk = gate_t.T
    g_final = norm_final_g.reshape(1, D_MODEL)
    y_p = _combine(dest, h, gate_tk, g_final, y_sorted, 0, T_P)
    y_s = _combine(dest, h, gate_tk, g_final, y_sorted, T_P, T_S)
    return (y_p.reshape(BATCH, SEQ, D_MODEL), y_s.reshape(DEC_BATCH, DEC_SEQ, D_MODEL),
            cs_p[None], ps_p[None], cs_s[None], ps_s[None])
```

```python
import functools

import jax
import jax.numpy as jnp
from jax import lax
from jax.experimental import pallas as pl
from jax.experimental.pallas import tpu as pltpu

D_MODEL = 4096
BATCH = 4
SEQ = 2048
DEC_BATCH = 16
DEC_SEQ = 64
D_CONV = D_MODEL // 2
D_POOL = D_MODEL // 2
CONV_HIST = 2
POOL_HIST = 15
POOL_GROUP_DIM = D_POOL // 4
N_EXPERTS = 32
TOP_K = 4
D_FF = D_MODEL
SWIGLU_ALPHA = 1.702
SWIGLU_LIMIT = 7.0
EPS = 1e-5

T_P = BATCH * SEQ
T_S = DEC_BATCH * DEC_SEQ
T = T_P + T_S
HALF = D_MODEL // 2

ROW_TILE = 1024
N_ROW_TILES = T // ROW_TILE
N_PROMPT_TILES = T_P // ROW_TILE
TC = 256
CONV_PAD = 8
POOL_PAD = 16

EB = 256
N_SLOTS = T * TOP_K
N_BLOCKS = N_SLOTS // EB + N_EXPERTS
N_ROWS = N_BLOCKS * EB
UNIT_BLOCKS = 6
UNIT_ROWS = UNIT_BLOCKS * EB
N_UNITS = N_EXPERTS + N_BLOCKS // UNIT_BLOCKS
TW = 256
NT1 = D_FF // TW
NT2 = D_MODEL // (2 * TW)
W_ROW_SPLIT = 8
GATHER_UNROLL = 4
GATHER_PER_TILE = UNIT_ROWS // NT2
ROUTE_TILE = 512
COMB_TOK = 64

VMEM_LIMIT = 56 * 1024 * 1024


def _cparams(sem):
    return pltpu.CompilerParams(dimension_semantics=sem, vmem_limit_bytes=VMEM_LIMIT)


def _norm_kernel(xp_ref, xs_ref, g_ref, o_ref):
    i = pl.program_id(0)

    def body(x):
        y = x * lax.rsqrt(jnp.mean(x * x, axis=-1, keepdims=True) + EPS)
        o_ref[...] = (y * g_ref[...]).astype(o_ref.dtype)

    @pl.when(i < T_P // 512)
    def _():
        body(xp_ref[...])

    @pl.when(i >= T_P // 512)
    def _():
        body(xs_ref[...])


def _norm(xp, xs, g):
    np_ = T_P // 512
    return pl.pallas_call(
        _norm_kernel,
        grid=(T // 512,),
        in_specs=[
            pl.BlockSpec((512, D_MODEL), lambda i: (jnp.minimum(i, np_ - 1), 0)),
            pl.BlockSpec((512, D_MODEL), lambda i: (jnp.maximum(i - np_, 0), 0)),
            pl.BlockSpec((1, D_MODEL), lambda i: (0, 0)),
        ],
        out_specs=pl.BlockSpec((512, D_MODEL), lambda i: (i, 0)),
        out_shape=jax.ShapeDtypeStruct((T, D_MODEL), jnp.bfloat16),
        compiler_params=_cparams(("arbitrary",)),
        name="mix_norm",
    )(xp, xs, g)


def _conv3(ev, ck):
    return ck[2:3, :] * ev + ck[1:2, :] * pltpu.roll(ev, 1, 0) + ck[0:1, :] * pltpu.roll(ev, 2, 0)


def _window_sum(eu, group):
    s2 = eu + pltpu.roll(eu, 1, 0)
    s4 = s2 + pltpu.roll(s2, 2, 0)
    s8 = s4 + pltpu.roll(s4, 4, 0)
    s16 = s8 + pltpu.roll(s8, 8, 0)
    return jnp.where(group == 0, s2, jnp.where(group == 1, s4, jnp.where(group == 2, s8, s16)))


def _proj_kernel(xn_ref, wb_ref, wc_ref, wx_ref, wu_ref, ck_ref, cc_ref, cp_ref,
                 z_ref, pooled_ref, csp_ref, psp_ref, css_ref, pss_ref,
                 vcarry, ucarry, vs, us):
    c = pl.program_id(0)
    i = pl.program_id(1)
    xn = xn_ref[...]
    dot = functools.partial(jnp.dot, preferred_element_type=jnp.float32)
    gate_b = dot(xn, wb_ref[...])
    v = dot(xn, wc_ref[...]) * dot(xn, wx_ref[...])
    u = dot(xn, wu_ref[...])
    ck = ck_ref[...]
    group = c // (POOL_GROUP_DIM // TC)
    window = jnp.left_shift(2, group)

    @pl.when(i < N_PROMPT_TILES)
    def _prompt():
        first = (i % 2) == 0
        vh = jnp.where(first, 0.0, vcarry[...])
        uh = jnp.where(first, 0.0, ucarry[...])
        conv = _conv3(jnp.concatenate([vh, v], axis=0), ck)[CONV_PAD:]
        z_ref[...] = (gate_b * conv).astype(z_ref.dtype)
        win = _window_sum(jnp.concatenate([uh, u], axis=0), group)[POOL_PAD:]
        pos = (i % 2) * ROW_TILE + lax.broadcasted_iota(jnp.int32, (ROW_TILE, 1), 0)
        cnt = jnp.minimum(pos + 1, window).astype(jnp.float32)
        pooled_ref[...] = (win * (1.0 / cnt) - u).astype(pooled_ref.dtype)
        vcarry[...] = v[ROW_TILE - CONV_PAD:]
        ucarry[...] = u[ROW_TILE - POOL_PAD:]
        csp_ref[0] = vcarry[CONV_PAD - CONV_HIST:, :]
        psp_ref[0] = ucarry[POOL_PAD - POOL_HIST:, :]

    @pl.when(i >= N_PROMPT_TILES)
    def _sample():
        ev, eu = [], []
        for s in range(DEC_BATCH):
            ev += [cc_ref[s], v[s * DEC_SEQ:(s + 1) * DEC_SEQ]]
            eu += [cp_ref[s], u[s * DEC_SEQ:(s + 1) * DEC_SEQ]]
        conv_e = _conv3(jnp.concatenate(ev, axis=0), ck)
        win_e = _window_sum(jnp.concatenate(eu, axis=0), group)
        lv, lu = CONV_PAD + DEC_SEQ, POOL_PAD + DEC_SEQ
        conv = jnp.concatenate([conv_e[s * lv + CONV_PAD:(s + 1) * lv] for s in range(DEC_BATCH)], axis=0)
        win = jnp.concatenate([win_e[s * lu + POOL_PAD:(s + 1) * lu] for s in range(DEC_BATCH)], axis=0)
        z_ref[...] = (gate_b * conv).astype(z_ref.dtype)
        inv = 1.0 / window.astype(jnp.float32)
        pooled_ref[...] = (win * inv - u).astype(pooled_ref.dtype)
        vs[...] = v
        us[...] = u
        for s in range(DEC_BATCH):
            end = (s + 1) * DEC_SEQ
            css_ref[s] = vs[end - CONV_HIST:end, :]
            pss_ref[s] = us[end - POOL_HIST:end, :]


def _proj(xn, w_in_bf, conv_k, cache_conv_pad, cache_pool_pad):
    nsec = D_CONV // TC
    pidx = lambda c, i: (jnp.minimum(i, N_PROMPT_TILES - 1) // 2, 0, c)
    return pl.pallas_call(
        _proj_kernel,
        grid=(nsec, N_ROW_TILES),
        in_specs=[
            pl.BlockSpec((ROW_TILE, D_MODEL), lambda c, i: (i, 0)),
            pl.BlockSpec((D_MODEL, TC), lambda c, i: (0, c)),
            pl.BlockSpec((D_MODEL, TC), lambda c, i: (0, nsec + c)),
            pl.BlockSpec((D_MODEL, TC), lambda c, i: (0, 2 * nsec + c)),
            pl.BlockSpec((D_MODEL, TC), lambda c, i: (0, 3 * nsec + c)),
            pl.BlockSpec((3, TC), lambda c, i: (0, c)),
            pl.BlockSpec((DEC_BATCH, CONV_PAD, TC), lambda c, i: (0, 0, c)),
            pl.BlockSpec((DEC_BATCH, POOL_PAD, TC), lambda c, i: (0, 0, c)),
        ],
        out_specs=[
            pl.BlockSpec((ROW_TILE, TC), lambda c, i: (i, c)),
            pl.BlockSpec((ROW_TILE, TC), lambda c, i: (i, c)),
            pl.BlockSpec((1, CONV_HIST, TC), pidx),
            pl.BlockSpec((1, POOL_HIST, TC), pidx),
            pl.BlockSpec((DEC_BATCH, CONV_HIST, TC), lambda c, i: (0, 0, c)),
            pl.BlockSpec((DEC_BATCH, POOL_HIST, TC), lambda c, i: (0, 0, c)),
        ],
        out_shape=[
            jax.ShapeDtypeStruct((T, D_CONV), jnp.bfloat16),
            jax.ShapeDtypeStruct((T, D_POOL), jnp.bfloat16),
            jax.ShapeDtypeStruct((BATCH, CONV_HIST, D_CONV), jnp.float32),
            jax.ShapeDtypeStruct((BATCH, POOL_HIST, D_POOL), jnp.float32),
            jax.ShapeDtypeStruct((DEC_BATCH, CONV_HIST, D_CONV), jnp.float32),
            jax.ShapeDtypeStruct((DEC_BATCH, POOL_HIST, D_POOL), jnp.float32),
        ],
        scratch_shapes=[
            pltpu.VMEM((CONV_PAD, TC), jnp.float32),
            pltpu.VMEM((POOL_PAD, TC), jnp.float32),
            pltpu.VMEM((ROW_TILE, TC), jnp.float32),
            pltpu.VMEM((ROW_TILE, TC), jnp.float32),
        ],
        compiler_params=_cparams(("arbitrary", "arbitrary")),
        name="mix_proj",
    )(xn, w_in_bf, w_in_bf, w_in_bf, w_in_bf, conv_k, cache_conv_pad, cache_pool_pad)


def _gates_kernel(xn_ref, w_ref, o_ref):
    g = jnp.dot(xn_ref[...], w_ref[...], preferred_element_type=jnp.float32)
    o_ref[...] = jax.nn.sigmoid(g).astype(o_ref.dtype)


def _gates(xn, w_in_bf):
    tn = 1024
    off = (3 * D_CONV + D_POOL) // tn
    return pl.pallas_call(
        _gates_kernel,
        grid=(2 * D_MODEL // tn, N_ROW_TILES),
        in_specs=[
            pl.BlockSpec((ROW_TILE, D_MODEL), lambda j, i: (i, 0)),
            pl.BlockSpec((D_MODEL, tn), lambda j, i: (0, off + j)),
        ],
        out_specs=pl.BlockSpec((ROW_TILE, tn), lambda j, i: (i, j)),
        out_shape=jax.ShapeDtypeStruct((T, 2 * D_MODEL), jnp.bfloat16),
        compiler_params=_cparams(("arbitrary", "arbitrary")),
        name="mix_gates",
    )(xn, w_in_bf)


def _merge_kernel(z_ref, wco_ref, p_ref, wpm_ref, scale_ref, ga_ref, gb_ref, o_ref):
    y_conv = jnp.dot(z_ref[...], wco_ref[...], preferred_element_type=jnp.float32)
    y_pool = jnp.dot(p_ref[...], wpm_ref[0], preferred_element_type=jnp.float32) * scale_ref[...]
    mix = ga_ref[...].astype(jnp.float32) * y_conv + gb_ref[...].astype(jnp.float32) * y_pool
    o_ref[...] = mix.astype(o_ref.dtype)


def _merge(z, wco_bf, pooled, wpm_bf, pool_scale, sg):
    tn = D_MODEL // 4
    return pl.pallas_call(
        _merge_kernel,
        grid=(4, N_ROW_TILES),
        in_specs=[
            pl.BlockSpec((ROW_TILE, D_CONV), lambda j, i: (i, 0)),
            pl.BlockSpec((D_CONV, tn), lambda j, i: (0, j)),
            pl.BlockSpec((ROW_TILE, POOL_GROUP_DIM), lambda j, i: (i, j)),
            pl.BlockSpec((1, POOL_GROUP_DIM, tn), lambda j, i: (j, 0, 0)),
            pl.BlockSpec((1, tn), lambda j, i: (0, j)),
            pl.BlockSpec((ROW_TILE, tn), lambda j, i: (i, j)),
            pl.BlockSpec((ROW_TILE, tn), lambda j, i: (i, 4 + j)),
        ],
        out_specs=pl.BlockSpec((ROW_TILE, tn), lambda j, i: (i, j)),
        out_shape=jax.ShapeDtypeStruct((T, D_MODEL), jnp.bfloat16),
        compiler_params=_cparams(("arbitrary", "arbitrary")),
        name="mix_merge",
    )(z, wco_bf, pooled, wpm_bf, pool_scale, sg, sg)


def _oproj_kernel(m_ref, w_ref, xp_ref, xs_ref, o_ref):
    i = pl.program_id(1)
    a = jnp.dot(m_ref[...], w_ref[...], preferred_element_type=jnp.float32)

    @pl.when(i < T_P // 512)
    def _():
        o_ref[...] = xp_ref[...] + a

    @pl.when(i >= T_P // 512)
    def _():
        o_ref[...] = xs_ref[...] + a


def _oproj(mix, w_o_bf, xp, xs):
    tn = 1024
    np_ = T_P // 512
    return pl.pallas_call(
        _oproj_kernel,
        grid=(D_MODEL // tn, T // 512),
        in_specs=[
            pl.BlockSpec((512, D_MODEL), lambda j, i: (i, 0)),
            pl.BlockSpec((D_MODEL, tn), lambda j, i: (0, j)),
            pl.BlockSpec((512, tn), lambda j, i: (jnp.minimum(i, np_ - 1), j)),
            pl.BlockSpec((512, tn), lambda j, i: (jnp.maximum(i - np_, 0), j)),
        ],
        out_specs=pl.BlockSpec((512, tn), lambda j, i: (i, j)),
        out_shape=jax.ShapeDtypeStruct((T, D_MODEL), jnp.float32),
        compiler_params=_cparams(("arbitrary", "arbitrary")),
        name="mix_oproj",
    )(mix, w_o_bf, xp, xs)


def _f32_bits(x):
    return pltpu.bitcast(x, jnp.uint32)


def _route_kernel(h_ref, g_ref, wr_ref, br_ref, hn_ref, idx_ref, gate_ref, rank_ref, hist_ref):
    h = h_ref[...]
    hn = h * lax.rsqrt(jnp.mean(h * h, axis=-1, keepdims=True) + EPS) * g_ref[...]
    lo = _f32_bits(hn[:, :HALF].astype(jnp.bfloat16).astype(jnp.float32))
    hi = _f32_bits(hn[:, HALF:].astype(jnp.bfloat16).astype(jnp.float32))
    hn_ref[...] = jnp.right_shift(lo, jnp.uint32(16)) | (hi & jnp.uint32(0xFFFF0000))
    logits = lax.dot_general(wr_ref[...], hn, (((1,), (1,)), ((), ())),
                             precision=lax.Precision.HIGHEST,
                             preferred_element_type=jnp.float32) + br_ref[...]
    eid = lax.broadcasted_iota(jnp.int32, logits.shape, 0)
    vals, ids = [], []
    for _ in range(TOP_K):
        m = jnp.max(logits, axis=0, keepdims=True)
        sel = jnp.min(jnp.where(logits == m, eid, N_EXPERTS), axis=0, keepdims=True)
        vals.append(m)
        ids.append(sel)
        logits = jnp.where(eid == sel, -jnp.inf, logits)
    ex = [jnp.exp(v - vals[0]) for v in vals]
    den = ex[0] + ex[1] + ex[2] + ex[3]
    for k in range(TOP_K):
        idx_ref[k:k + 1, :] = ids[k]
        gate_ref[k:k + 1, :] = ex[k] / den
    tm = logits.shape[1]
    tri = (lax.broadcasted_iota(jnp.int32, (tm, tm), 0)
           <= lax.broadcasted_iota(jnp.int32, (tm, tm), 1)).astype(jnp.bfloat16)
    run = jnp.zeros((N_EXPERTS, 1), jnp.float32)
    for k in range(TOP_K):
        oh = (eid == ids[k]).astype(jnp.float32)
        seen = jnp.dot(oh.astype(jnp.bfloat16), tri, preferred_element_type=jnp.float32)
        rank = jnp.sum(oh * (seen - 1.0 + run), axis=0, keepdims=True)
        rank_ref[k:k + 1, :] = rank.astype(jnp.int32)
        run = run + jnp.sum(oh, axis=1, keepdims=True)
    hist_ref[0] = jnp.broadcast_to(run, (N_EXPERTS, 128)).astype(jnp.int32)


def _route(h, g, wr_t, br):
    tm = ROUTE_TILE
    return pl.pallas_call(
        _route_kernel,
        grid=(T // tm,),
        in_specs=[
            pl.BlockSpec((tm, D_MODEL), lambda i: (i, 0)),
            pl.BlockSpec((1, D_MODEL), lambda i: (0, 0)),
            pl.BlockSpec((N_EXPERTS, D_MODEL), lambda i: (0, 0)),
            pl.BlockSpec((N_EXPERTS, 1), lambda i: (0, 0)),
        ],
        out_specs=[
            pl.BlockSpec((tm, HALF), lambda i: (i, 0)),
            pl.BlockSpec((TOP_K, tm), lambda i: (0, i)),
            pl.BlockSpec((TOP_K, tm), lambda i: (0, i)),
            pl.BlockSpec((TOP_K, tm), lambda i: (0, i)),
            pl.BlockSpec((1, N_EXPERTS, 128), lambda i: (i, 0, 0)),
        ],
        out_shape=[
            jax.ShapeDtypeStruct((T, HALF), jnp.uint32),
            jax.ShapeDtypeStruct((TOP_K, T), jnp.int32),
            jax.ShapeDtypeStruct((TOP_K, T), jnp.float32),
            jax.ShapeDtypeStruct((TOP_K, T), jnp.int32),
            jax.ShapeDtypeStruct((T // tm, N_EXPERTS, 128), jnp.int32),
        ],
        compiler_params=_cparams(("arbitrary",)),
        name="ffn_route",
    )(h, g, wr_t, br)


def _unpack_rows(xw):
    lo = pltpu.bitcast(jnp.left_shift(xw, jnp.uint32(16)), jnp.float32).astype(jnp.bfloat16)
    hi = pltpu.bitcast(xw & jnp.uint32(0xFFFF0000), jnp.float32).astype(jnp.bfloat16)
    return lo, hi


def _experts_kernel(ue_ref, ub_ref, un_ref, meta_ref, tok_ref,
                    bgu_ref, bd_ref, hn_hbm, wgu_hbm, wd_hbm, y_hbm,
                    xbuf, act, wstage, wbf, ystage, gsem, wsem, ysem):
    u = pl.program_id(0)
    n_units = meta_ref[0]
    n_used_blocks = meta_ref[1]
    dot = functools.partial(jnp.dot, preferred_element_type=jnp.float32)

    def gather_copy(unit, r):
        tok = tok_ref[ub_ref[unit] * EB + r]
        return pltpu.make_async_copy(hn_hbm.at[pl.ds(tok, 1), :], xbuf.at[pl.ds(r, 1), :], gsem)

    def start_gather(unit):
        def body(q, carry):
            for d in range(GATHER_UNROLL):
                gather_copy(unit, q * GATHER_UNROLL + d).start()
            return carry
        lax.fori_loop(0, UNIT_ROWS // GATHER_UNROLL, body, 0)

    def start_gather_part(unit, part):
        for d in range(GATHER_PER_TILE):
            gather_copy(unit, part * GATHER_PER_TILE + d).start()

    def wait_gather():
        pltpu.make_async_copy(hn_hbm.at[pl.ds(0, UNIT_ROWS), :], xbuf, gsem).wait()

    def tile_copies(w_hbm, e, cols, slot):
        out = []
        chunk = D_MODEL // W_ROW_SPLIT
        for part, c0 in enumerate(cols):
            for kh in range(W_ROW_SPLIT):
                rows = pl.ds(kh * chunk, chunk)
                out.append(pltpu.make_async_copy(w_hbm.at[e, rows, pl.ds(c0, TW)],
                                                 wstage.at[slot, part, rows, :], wsem.at[slot]))
        return out

    def gate_up_copies(e, j, slot):
        c0 = pl.multiple_of(j * TW, TW)
        return tile_copies(wgu_hbm, e, (c0, D_FF + c0), slot)

    def down_copies(e, j, slot):
        c0 = pl.multiple_of(j * 2 * TW, 2 * TW)
        return tile_copies(wd_hbm, e, (c0, c0 + TW), slot)

    def start(copies):
        for c in copies:
            c.start()

    def wait_tile(slot):
        pltpu.make_async_copy(wgu_hbm.at[pl.ds(0, 2), :, pl.ds(0, TW)], wstage.at[slot], wsem.at[slot]).wait()

    def cast_tile(slot):
        wbf[:, :TW] = wstage[slot, 0].astype(jnp.bfloat16)
        wbf[:, TW:] = wstage[slot, 1].astype(jnp.bfloat16)

    def y_copy(block, j, ys):
        r0 = pl.multiple_of(block * EB, EB)
        c0 = pl.multiple_of(j * 2 * TW, 2 * TW)
        return pltpu.make_async_copy(ystage.at[ys], y_hbm.at[pl.ds(r0, UNIT_ROWS), pl.ds(c0, 2 * TW)],
                                     ysem.at[ys])

    def zero_copy(block, j):
        r0 = pl.multiple_of(block * EB, EB)
        c0 = pl.multiple_of(j * 2 * TW, 2 * TW)
        return pltpu.make_async_copy(ystage.at[0, pl.ds(0, EB), :],
                                     y_hbm.at[pl.ds(r0, EB), pl.ds(c0, 2 * TW)], ysem.at[0])

    def gate_up_rows(r0, j, bg, bl):
        lo, hi = _unpack_rows(xbuf[pl.ds(r0, EB), :])
        hcat = dot(lo, wbf[:HALF, :]) + dot(hi, wbf[HALF:, :])
        glu = jnp.minimum(hcat[:, :TW] + bg, SWIGLU_LIMIT)
        lin = jnp.clip(hcat[:, TW:] + bl, -SWIGLU_LIMIT, SWIGLU_LIMIT)
        a = glu * jax.nn.sigmoid(SWIGLU_ALPHA * glu) * (lin + 1.0)
        act[j, pl.ds(r0, EB), :] = a.astype(act.dtype)

    def down_rows(r0, ys, bd):
        a = jnp.concatenate([act[jj, pl.ds(r0, EB), :] for jj in range(NT1)], axis=1)
        ystage[ys, pl.ds(r0, EB), :] = dot(a, wbf[...]) + bd

    def for_blocks_after_first(nb, block_fn):
        rest = nb - 1
        for run in (4, 2, 1):
            @pl.when((rest & run) != 0)
            def _(run=run):
                first = 1 + (rest & ~(2 * run - 1))
                for q in range(run):
                    block_fn(pl.multiple_of((first + q) * EB, EB))

    @pl.when(u == 0)
    def _prologue():
        ystage[...] = jnp.zeros(ystage.shape, ystage.dtype)
        start_gather(0)
        start(gate_up_copies(ue_ref[0], 0, 0))

    @pl.when(u < n_units)
    def _unit():
        e = ue_ref[u]
        b0 = ub_ref[u]
        nb = un_ref[u]
        has_next = u + 1 < n_units
        wait_gather()

        def gate_up_tile(j, carry):
            slot = j % 2
            wait_tile(slot)

            @pl.when(j + 1 < NT1)
            def _():
                start(gate_up_copies(e, j + 1, 1 - slot))

            @pl.when(j + 1 == NT1)
            def _():
                start(down_copies(e, 0, 1 - slot))

            bg = bgu_ref[0, pl.ds(j, 1), :]
            bl = bgu_ref[0, pl.ds(NT1 + j, 1), :]
            cast_tile(slot)
            gate_up_rows(0, j, bg, bl)
            for_blocks_after_first(nb, lambda r0: gate_up_rows(r0, j, bg, bl))
            return carry

        lax.fori_loop(0, NT1, gate_up_tile, 0)

        def down_tile(j, carry):
            slot = j % 2
            wait_tile(slot)

            @pl.when(j + 1 < NT2)
            def _():
                start(down_copies(e, j + 1, 1 - slot))

            @pl.when((j + 1 == NT2) & has_next)
            def _():
                start(gate_up_copies(ue_ref[u + 1], 0, 1 - slot))

            bd = bd_ref[0, pl.ds(j, 1), :]
            ys = j % 2

            @pl.when(j >= 2)
            def _():
                y_copy(b0, j, ys).wait()

            cast_tile(slot)
            start_gather_part(u + 1, j)
            down_rows(0, ys, bd)
            for_blocks_after_first(nb, lambda r0: down_rows(r0, ys, bd))
            y_copy(b0, j, ys).start()
            return carry

        lax.fori_loop(0, NT2, down_tile, 0)
        y_copy(b0, 0, 0).wait()
        y_copy(b0, 0, 1).wait()

        @pl.when(jnp.logical_not(has_next))
        def _():
            wait_gather()

    @pl.when(u == N_UNITS - 1)
    def _zero_unused_rows():
        ystage[0] = jnp.zeros(ystage.shape[1:], ystage.dtype)

        def body(b, carry):
            for j in range(NT2):
                zero_copy(b, j).start()
            for j in range(NT2):
                zero_copy(b, j).wait()
            return carry

        lax.fori_loop(n_used_blocks, N_BLOCKS + UNIT_BLOCKS, body, 0)


def _experts(unit_tables, row_tok, hn_packed, w_gate_up, b_gate_up, w_down, b_down):
    any_spec = pl.BlockSpec(memory_space=pl.ANY)
    return pl.pallas_call(
        _experts_kernel,
        grid_spec=pltpu.PrefetchScalarGridSpec(
            num_scalar_prefetch=5,
            grid=(N_UNITS,),
            in_specs=[
                pl.BlockSpec((1, 2 * NT1, TW), lambda u, ue, ub, un, meta, tok: (ue[u], 0, 0)),
                pl.BlockSpec((1, NT2, 2 * TW), lambda u, ue, ub, un, meta, tok: (ue[u], 0, 0)),
                any_spec, any_spec, any_spec,
            ],
            out_specs=any_spec,
            scratch_shapes=[
                pltpu.VMEM((UNIT_ROWS, HALF), jnp.uint32),
                pltpu.VMEM((NT1, UNIT_ROWS, TW), jnp.bfloat16),
                pltpu.VMEM((2, 2, D_MODEL, TW), jnp.float32),
                pltpu.VMEM((D_MODEL, 2 * TW), jnp.bfloat16),
                pltpu.VMEM((2, UNIT_ROWS, 2 * TW), jnp.float32),
                pltpu.SemaphoreType.DMA(()),
                pltpu.SemaphoreType.DMA((2,)),
                pltpu.SemaphoreType.DMA((2,)),
            ],
        ),
        out_shape=jax.ShapeDtypeStruct((N_ROWS + UNIT_ROWS, D_MODEL), jnp.float32),
        compiler_params=_cparams(("arbitrary",)),
        name="ffn_experts",
    )(*unit_tables, row_tok, b_gate_up, b_down, hn_packed, w_gate_up, w_down)


def _combine_kernel(pos_ref, h_ref, gate_ref, g_ref, y_hbm, o_ref, rows, sem, *, tok0):
    i = pl.program_id(0)
    slot = i % 2

    def start_rows(step, dst_slot):
        base = tok0 + step * COMB_TOK

        def body(t, carry):
            for k in range(TOP_K):
                p = pos_ref[k * T + base + t]
                pltpu.make_async_copy(y_hbm.at[pl.ds(p, 1), :], rows.at[dst_slot, k, pl.ds(t, 1), :],
                                      sem.at[dst_slot]).start()
            return carry

        lax.fori_loop(0, COMB_TOK, body, 0)

    @pl.when(i == 0)
    def _():
        start_rows(0, 0)

    @pl.when(i + 1 < pl.num_programs(0))
    def _():
        start_rows(i + 1, 1 - slot)

    for k in range(TOP_K):
        pltpu.make_async_copy(y_hbm.at[pl.ds(0, COMB_TOK), :], rows.at[slot, k], sem.at[slot]).wait()
    gate = gate_ref[...]
    acc = rows[slot, 0] * gate[:, 0:1]
    for k in range(1, TOP_K):
        acc = acc + rows[slot, k] * gate[:, k:k + 1]
    x = h_ref[...] + acc
    y = x * lax.rsqrt(jnp.mean(x * x, axis=-1, keepdims=True) + EPS)
    o_ref[...] = y * g_ref[...]


def _combine(pos_flat, h, gate_tk, g_final, y_sorted, tok0, n_tok):
    blk0 = tok0 // COMB_TOK
    return pl.pallas_call(
        functools.partial(_combine_kernel, tok0=tok0),
        grid_spec=pltpu.PrefetchScalarGridSpec(
            num_scalar_prefetch=1,
            grid=(n_tok // COMB_TOK,),
            in_specs=[
                pl.BlockSpec((COMB_TOK, D_MODEL), lambda i, pos: (blk0 + i, 0)),
                pl.BlockSpec((COMB_TOK, TOP_K), lambda i, pos: (blk0 + i, 0)),
                pl.BlockSpec((1, D_MODEL), lambda i, pos: (0, 0)),
                pl.BlockSpec(memory_space=pl.ANY),
            ],
            out_specs=pl.BlockSpec((COMB_TOK, D_MODEL), lambda i, pos: (i, 0)),
            scratch_shapes=[
                pltpu.VMEM((2, TOP_K, COMB_TOK, D_MODEL), jnp.float32),
                pltpu.SemaphoreType.DMA((2,)),
            ],
        ),
        out_shape=jax.ShapeDtypeStruct((n_tok, D_MODEL), jnp.float32),
        compiler_params=_cparams(("arbitrary",)),
        name="ffn_combine",
    )(pos_flat, h, gate_tk, g_final, y_sorted)


def _routing(idx_t, rank_t, hist):
    i32 = jnp.int32
    tile_off = jnp.cumsum(hist, axis=0) - hist
    counts = jnp.sum(hist, axis=0)
    nblk = (counts + EB - 1) // EB
    blk_end = jnp.cumsum(nblk)
    blk_start = blk_end - nblk
    base_tab = tile_off + blk_start[None, :] * EB
    n_tiles = T // ROUTE_TILE
    onehot = idx_t.reshape(TOP_K, n_tiles, ROUTE_TILE, 1) == jnp.arange(N_EXPERTS, dtype=i32)
    base = jnp.sum(jnp.where(onehot, base_tab[None, :, None, :], 0), axis=-1)
    dest = (base.reshape(TOP_K, T) + rank_t).reshape(-1).astype(i32)
    tok = jnp.tile(jnp.arange(T, dtype=i32), TOP_K)
    row_tok = jnp.zeros((N_ROWS + UNIT_ROWS,), i32).at[dest].set(tok)

    n_unit_e = (nblk + UNIT_BLOCKS - 1) // UNIT_BLOCKS
    unit_end = jnp.cumsum(n_unit_e)
    n_units = unit_end[-1]
    u = jnp.minimum(jnp.arange(N_UNITS, dtype=i32), n_units - 1)
    e = jnp.minimum(jnp.sum(unit_end[None, :] <= u[:, None], axis=1), N_EXPERTS - 1).astype(i32)
    local = u - (unit_end[e] - n_unit_e[e])
    b0 = blk_start[e] + local * UNIT_BLOCKS
    nb = jnp.minimum(nblk[e] - local * UNIT_BLOCKS, UNIT_BLOCKS)
    meta = jnp.stack([n_units, blk_end[-1]]).astype(i32)
    return row_tok, dest, (e, b0.astype(i32), nb.astype(i32), meta)


def kernel(x_prompt, x_sample, cache_conv, cache_pool, norm_mix_g, w_in, conv_k, w_conv_out, w_pool_map,
           pool_scale, w_o, norm_ffn_g, w_router, b_router, w_gate_up, b_gate_up, w_down, b_down,
           norm_final_g):
    bf = jnp.bfloat16
    xp = x_prompt.reshape(T_P, D_MODEL)
    xs = x_sample.reshape(T_S, D_MODEL)
    cc_pad = jnp.pad(cache_conv[0], ((0, 0), (CONV_PAD - CONV_HIST, 0), (0, 0)))
    cp_pad = jnp.pad(cache_pool[0], ((0, 0), (POOL_PAD - POOL_HIST, 0), (0, 0)))

    xn = _norm(xp, xs, norm_mix_g)
    w_in_bf = w_in[0].astype(bf)
    z, pooled, cs_p, ps_p, cs_s, ps_s = _proj(xn, w_in_bf, conv_k[0], cc_pad, cp_pad)
    sg = _gates(xn, w_in_bf)
    mix = _merge(z, w_conv_out[0].astype(bf), pooled, w_pool_map[0].astype(bf), pool_scale, sg)
    h = _oproj(mix, w_o[0].astype(bf), xp, xs)

    hn_packed, idx_t, gate_t, rank_t, hist = _route(h, norm_ffn_g, w_router[0].T,
                                                    b_router[0].reshape(N_EXPERTS, 1))
    row_tok, dest, unit_tables = _routing(idx_t, rank_t, hist[:, :, 0])
    y_sorted = _experts(unit_tables, row_tok, hn_packed,
                        w_gate_up[0], b_gate_up[0].reshape(N_EXPERTS, 2 * NT1, TW),
                        w_down[0], b_down[0].reshape(N_EXPERTS, NT2, 2 * TW))

    gate_tk = gate_t.T
    g_final = norm_final_g.reshape(1, D_MODEL)
    y_p = _combine(dest, h, gate_tk, g_final, y_sorted, 0, T_P)
    y_s = _combine(dest, h, gate_tk, g_final, y_sorted, T_P, T_S)
    return (y_p.reshape(BATCH, SEQ, D_MODEL), y_s.reshape(DEC_BATCH, DEC_SEQ, D_MODEL),
            cs_p[None], ps_p[None], cs_s[None], ps_s[None])
```

```python
import functools

import jax
import jax.numpy as jnp
from jax import lax
from jax.experimental import pallas as pl
from jax.experimental.pallas import tpu as pltpu

D_MODEL = 4096
BATCH = 4
SEQ = 2048
DEC_BATCH = 16
DEC_SEQ = 64
D_CONV = D_MODEL // 2
D_POOL = D_MODEL // 2
CONV_HIST = 2
POOL_HIST = 15
POOL_GROUP_DIM = D_POOL // 4
N_EXPERTS = 32
TOP_K = 4
D_FF = D_MODEL
SWIGLU_ALPHA = 1.702
SWIGLU_LIMIT = 7.0
EPS = 1e-5

T_P = BATCH * SEQ
T_S = DEC_BATCH * DEC_SEQ
T = T_P + T_S
HALF = D_MODEL // 2

ROW_TILE = 1024
N_ROW_TILES = T // ROW_TILE
N_PROMPT_TILES = T_P // ROW_TILE
TC = 256
CONV_PAD = 8
POOL_PAD = 16

EB = 256
N_SLOTS = T * TOP_K
N_BLOCKS = N_SLOTS // EB + N_EXPERTS
N_ROWS = N_BLOCKS * EB
UNIT_BLOCKS = 6
UNIT_ROWS = UNIT_BLOCKS * EB
N_UNITS = N_EXPERTS + N_BLOCKS // UNIT_BLOCKS
TW = 256
NT1 = D_FF // TW
NT2 = D_MODEL // (2 * TW)
W_ROW_SPLIT = 8
GATHER_UNROLL = 4
GATHER_PER_TILE = UNIT_ROWS // NT2
ROUTE_TILE = 512
COMB_TOK = 128

VMEM_LIMIT = 56 * 1024 * 1024


def _cparams(sem):
    return pltpu.CompilerParams(dimension_semantics=sem, vmem_limit_bytes=VMEM_LIMIT)


def _norm_kernel(xp_ref, xs_ref, g_ref, o_ref):
    i = pl.program_id(0)

    def body(x):
        y = x * lax.rsqrt(jnp.mean(x * x, axis=-1, keepdims=True) + EPS)
        o_ref[...] = (y * g_ref[...]).astype(o_ref.dtype)

    @pl.when(i < T_P // 512)
    def _():
        body(xp_ref[...])

    @pl.when(i >= T_P // 512)
    def _():
        body(xs_ref[...])


def _norm(xp, xs, g):
    np_ = T_P // 512
    return pl.pallas_call(
        _norm_kernel,
        grid=(T // 512,),
        in_specs=[
            pl.BlockSpec((512, D_MODEL), lambda i: (jnp.minimum(i, np_ - 1), 0)),
            pl.BlockSpec((512, D_MODEL), lambda i: (jnp.maximum(i - np_, 0), 0)),
            pl.BlockSpec((1, D_MODEL), lambda i: (0, 0)),
        ],
        out_specs=pl.BlockSpec((512, D_MODEL), lambda i: (i, 0)),
        out_shape=jax.ShapeDtypeStruct((T, D_MODEL), jnp.bfloat16),
        compiler_params=_cparams(("arbitrary",)),
        name="mix_norm",
    )(xp, xs, g)


def _conv3(ev, ck):
    return ck[2:3, :] * ev + ck[1:2, :] * pltpu.roll(ev, 1, 0) + ck[0:1, :] * pltpu.roll(ev, 2, 0)


def _window_sum(eu, group):
    s2 = eu + pltpu.roll(eu, 1, 0)
    s4 = s2 + pltpu.roll(s2, 2, 0)
    s8 = s4 + pltpu.roll(s4, 4, 0)
    s16 = s8 + pltpu.roll(s8, 8, 0)
    return jnp.where(group == 0, s2, jnp.where(group == 1, s4, jnp.where(group == 2, s8, s16)))


def _proj_kernel(xn_ref, wb_ref, wc_ref, wx_ref, wu_ref, ck_ref, cc_ref, cp_ref,
                 z_ref, pooled_ref, csp_ref, psp_ref, css_ref, pss_ref,
                 vcarry, ucarry, vs, us):
    c = pl.program_id(0)
    i = pl.program_id(1)
    xn = xn_ref[...]
    dot = functools.partial(jnp.dot, preferred_element_type=jnp.float32)
    gate_b = dot(xn, wb_ref[...])
    v = dot(xn, wc_ref[...]) * dot(xn, wx_ref[...])
    u = dot(xn, wu_ref[...])
    ck = ck_ref[...]
    group = c // (POOL_GROUP_DIM // TC)
    window = jnp.left_shift(2, group)

    @pl.when(i < N_PROMPT_TILES)
    def _prompt():
        first = (i % 2) == 0
        vh = jnp.where(first, 0.0, vcarry[...])
        uh = jnp.where(first, 0.0, ucarry[...])
        conv = _conv3(jnp.concatenate([vh, v], axis=0), ck)[CONV_PAD:]
        z_ref[...] = (gate_b * conv).astype(z_ref.dtype)
        win = _window_sum(jnp.concatenate([uh, u], axis=0), group)[POOL_PAD:]
        pos = (i % 2) * ROW_TILE + lax.broadcasted_iota(jnp.int32, (ROW_TILE, 1), 0)
        cnt = jnp.minimum(pos + 1, window).astype(jnp.float32)
        pooled_ref[...] = (win * (1.0 / cnt) - u).astype(pooled_ref.dtype)
        vcarry[...] = v[ROW_TILE - CONV_PAD:]
        ucarry[...] = u[ROW_TILE - POOL_PAD:]
        csp_ref[0] = vcarry[CONV_PAD - CONV_HIST:, :]
        psp_ref[0] = ucarry[POOL_PAD - POOL_HIST:, :]

    @pl.when(i >= N_PROMPT_TILES)
    def _sample():
        ev, eu = [], []
        for s in range(DEC_BATCH):
            ev += [cc_ref[s], v[s * DEC_SEQ:(s + 1) * DEC_SEQ]]
            eu += [cp_ref[s], u[s * DEC_SEQ:(s + 1) * DEC_SEQ]]
        conv_e = _conv3(jnp.concatenate(ev, axis=0), ck)
        win_e = _window_sum(jnp.concatenate(eu, axis=0), group)
        lv, lu = CONV_PAD + DEC_SEQ, POOL_PAD + DEC_SEQ
        conv = jnp.concatenate([conv_e[s * lv + CONV_PAD:(s + 1) * lv] for s in range(DEC_BATCH)], axis=0)
        win = jnp.concatenate([win_e[s * lu + POOL_PAD:(s + 1) * lu] for s in range(DEC_BATCH)], axis=0)
        z_ref[...] = (gate_b * conv).astype(z_ref.dtype)
        inv = 1.0 / window.astype(jnp.float32)
        pooled_ref[...] = (win * inv - u).astype(pooled_ref.dtype)
        vs[...] = v
        us[...] = u
        for s in range(DEC_BATCH):
            end = (s + 1) * DEC_SEQ
            css_ref[s] = vs[end - CONV_HIST:end, :]
            pss_ref[s] = us[end - POOL_HIST:end, :]


def _proj(xn, w_in_bf, conv_k, cache_conv_pad, cache_pool_pad):
    nsec = D_CONV // TC
    pidx = lambda c, i: (jnp.minimum(i, N_PROMPT_TILES - 1) // 2, 0, c)
    return pl.pallas_call(
        _proj_kernel,
        grid=(nsec, N_ROW_TILES),
        in_specs=[
            pl.BlockSpec((ROW_TILE, D_MODEL), lambda c, i: (i, 0)),
            pl.BlockSpec((D_MODEL, TC), lambda c, i: (0, c)),
            pl.BlockSpec((D_MODEL, TC), lambda c, i: (0, nsec + c)),
            pl.BlockSpec((D_MODEL, TC), lambda c, i: (0, 2 * nsec + c)),
            pl.BlockSpec((D_MODEL, TC), lambda c, i: (0, 3 * nsec + c)),
            pl.BlockSpec((3, TC), lambda c, i: (0, c)),
            pl.BlockSpec((DEC_BATCH, CONV_PAD, TC), lambda c, i: (0, 0, c)),
            pl.BlockSpec((DEC_BATCH, POOL_PAD, TC), lambda c, i: (0, 0, c)),
        ],
        out_specs=[
            pl.BlockSpec((ROW_TILE, TC), lambda c, i: (i, c)),
            pl.BlockSpec((ROW_TILE, TC), lambda c, i: (i, c)),
            pl.BlockSpec((1, CONV_HIST, TC), pidx),
            pl.BlockSpec((1, POOL_HIST, TC), pidx),
            pl.BlockSpec((DEC_BATCH, CONV_HIST, TC), lambda c, i: (0, 0, c)),
            pl.BlockSpec((DEC_BATCH, POOL_HIST, TC), lambda c, i: (0, 0, c)),
        ],
        out_shape=[
            jax.ShapeDtypeStruct((T, D_CONV), jnp.bfloat16),
            jax.ShapeDtypeStruct((T, D_POOL), jnp.bfloat16),
            jax.ShapeDtypeStruct((BATCH, CONV_HIST, D_CONV), jnp.float32),
            jax.ShapeDtypeStruct((BATCH, POOL_HIST, D_POOL), jnp.float32),
            jax.ShapeDtypeStruct((DEC_BATCH, CONV_HIST, D_CONV), jnp.float32),
            jax.ShapeDtypeStruct((DEC_BATCH, POOL_HIST, D_POOL), jnp.float32),
        ],
        scratch_shapes=[
            pltpu.VMEM((CONV_PAD, TC), jnp.float32),
            pltpu.VMEM((POOL_PAD, TC), jnp.float32),
            pltpu.VMEM((ROW_TILE, TC), jnp.float32),
            pltpu.VMEM((ROW_TILE, TC), jnp.float32),
        ],
        compiler_params=_cparams(("arbitrary", "arbitrary")),
        name="mix_proj",
    )(xn, w_in_bf, w_in_bf, w_in_bf, w_in_bf, conv_k, cache_conv_pad, cache_pool_pad)


def _gates_kernel(xn_ref, w_ref, o_ref):
    g = jnp.dot(xn_ref[...], w_ref[...], preferred_element_type=jnp.float32)
    o_ref[...] = jax.nn.sigmoid(g).astype(o_ref.dtype)


def _gates(xn, w_in_bf):
    tn = 1024
    off = (3 * D_CONV + D_POOL) // tn
    return pl.pallas_call(
        _gates_kernel,
        grid=(2 * D_MODEL // tn, N_ROW_TILES),
        in_specs=[
            pl.BlockSpec((ROW_TILE, D_MODEL), lambda j, i: (i, 0)),
            pl.BlockSpec((D_MODEL, tn), lambda j, i: (0, off + j)),
        ],
        out_specs=pl.BlockSpec((ROW_TILE, tn), lambda j, i: (i, j)),
        out_shape=jax.ShapeDtypeStruct((T, 2 * D_MODEL), jnp.bfloat16),
        compiler_params=_cparams(("arbitrary", "arbitrary")),
        name="mix_gates",
    )(xn, w_in_bf)


def _merge_kernel(z_ref, wco_ref, p_ref, wpm_ref, scale_ref, ga_ref, gb_ref, o_ref):
    y_conv = jnp.dot(z_ref[...], wco_ref[...], preferred_element_type=jnp.float32)
    y_pool = jnp.dot(p_ref[...], wpm_ref[0], preferred_element_type=jnp.float32) * scale_ref[...]
    mix = ga_ref[...].astype(jnp.float32) * y_conv + gb_ref[...].astype(jnp.float32) * y_pool
    o_ref[...] = mix.astype(o_ref.dtype)


def _merge(z, wco_bf, pooled, wpm_bf, pool_scale, sg):
    tn = D_MODEL // 4
    return pl.pallas_call(
        _merge_kernel,
        grid=(4, N_ROW_TILES),
        in_specs=[
            pl.BlockSpec((ROW_TILE, D_CONV), lambda j, i: (i, 0)),
            pl.BlockSpec((D_CONV, tn), lambda j, i: (0, j)),
            pl.BlockSpec((ROW_TILE, POOL_GROUP_DIM), lambda j, i: (i, j)),
            pl.BlockSpec((1, POOL_GROUP_DIM, tn), lambda j, i: (j, 0, 0)),
            pl.BlockSpec((1, tn), lambda j, i: (0, j)),
            pl.BlockSpec((ROW_TILE, tn), lambda j, i: (i, j)),
            pl.BlockSpec((ROW_TILE, tn), lambda j, i: (i, 4 + j)),
        ],
        out_specs=pl.BlockSpec((ROW_TILE, tn), lambda j, i: (i, j)),
        out_shape=jax.ShapeDtypeStruct((T, D_MODEL), jnp.bfloat16),
        compiler_params=_cparams(("arbitrary", "arbitrary")),
        name="mix_merge",
    )(z, wco_bf, pooled, wpm_bf, pool_scale, sg, sg)


def _oproj_kernel(m_ref, w_ref, xp_ref, xs_ref, o_ref):
    i = pl.program_id(1)
    a = jnp.dot(m_ref[...], w_ref[...], preferred_element_type=jnp.float32)

    @pl.when(i < T_P // 512)
    def _():
        o_ref[...] = xp_ref[...] + a

    @pl.when(i >= T_P // 512)
    def _():
        o_ref[...] = xs_ref[...] + a


def _oproj(mix, w_o_bf, xp, xs):
    tn = 1024
    np_ = T_P // 512
    return pl.pallas_call(
        _oproj_kernel,
        grid=(D_MODEL // tn, T // 512),
        in_specs=[
            pl.BlockSpec((512, D_MODEL), lambda j, i: (i, 0)),
            pl.BlockSpec((D_MODEL, tn), lambda j, i: (0, j)),
            pl.BlockSpec((512, tn), lambda j, i: (jnp.minimum(i, np_ - 1), j)),
            pl.BlockSpec((512, tn), lambda j, i: (jnp.maximum(i - np_, 0), j)),
        ],
        out_specs=pl.BlockSpec((512, tn), lambda j, i: (i, j)),
        out_shape=jax.ShapeDtypeStruct((T, D_MODEL), jnp.float32),
        compiler_params=_cparams(("arbitrary", "arbitrary")),
        name="mix_oproj",
    )(mix, w_o_bf, xp, xs)


def _f32_bits(x):
    return pltpu.bitcast(x, jnp.uint32)


def _route_kernel(h_ref, g_ref, wr_ref, br_ref, hn_ref, idx_ref, gate_ref, rank_ref, hist_ref):
    h = h_ref[...]
    hn = h * lax.rsqrt(jnp.mean(h * h, axis=-1, keepdims=True) + EPS) * g_ref[...]
    lo = _f32_bits(hn[:, :HALF].astype(jnp.bfloat16).astype(jnp.float32))
    hi = _f32_bits(hn[:, HALF:].astype(jnp.bfloat16).astype(jnp.float32))
    hn_ref[...] = jnp.right_shift(lo, jnp.uint32(16)) | (hi & jnp.uint32(0xFFFF0000))
    logits = lax.dot_general(wr_ref[...], hn, (((1,), (1,)), ((), ())),
                             precision=lax.Precision.HIGHEST,
                             preferred_element_type=jnp.float32) + br_ref[...]
    eid = lax.broadcasted_iota(jnp.int32, logits.shape, 0)
    vals, ids = [], []
    for _ in range(TOP_K):
        m = jnp.max(logits, axis=0, keepdims=True)
        sel = jnp.min(jnp.where(logits == m, eid, N_EXPERTS), axis=0, keepdims=True)
        vals.append(m)
        ids.append(sel)
        logits = jnp.where(eid == sel, -jnp.inf, logits)
    ex = [jnp.exp(v - vals[0]) for v in vals]
    den = ex[0] + ex[1] + ex[2] + ex[3]
    for k in range(TOP_K):
        idx_ref[k:k + 1, :] = ids[k]
        gate_ref[k:k + 1, :] = ex[k] / den
    tm = logits.shape[1]
    tri = (lax.broadcasted_iota(jnp.int32, (tm, tm), 0)
           <= lax.broadcasted_iota(jnp.int32, (tm, tm), 1)).astype(jnp.bfloat16)
    run = jnp.zeros((N_EXPERTS, 1), jnp.float32)
    for k in range(TOP_K):
        oh = (eid == ids[k]).astype(jnp.float32)
        seen = jnp.dot(oh.astype(jnp.bfloat16), tri, preferred_element_type=jnp.float32)
        rank = jnp.sum(oh * (seen - 1.0 + run), axis=0, keepdims=True)
        rank_ref[k:k + 1, :] = rank.astype(jnp.int32)
        run = run + jnp.sum(oh, axis=1, keepdims=True)
    hist_ref[0] = jnp.broadcast_to(run, (N_EXPERTS, 128)).astype(jnp.int32)


def _route(h, g, wr_t, br):
    tm = ROUTE_TILE
    return pl.pallas_call(
        _route_kernel,
        grid=(T // tm,),
        in_specs=[
            pl.BlockSpec((tm, D_MODEL), lambda i: (i, 0)),
            pl.BlockSpec((1, D_MODEL), lambda i: (0, 0)),
            pl.BlockSpec((N_EXPERTS, D_MODEL), lambda i: (0, 0)),
            pl.BlockSpec((N_EXPERTS, 1), lambda i: (0, 0)),
        ],
        out_specs=[
            pl.BlockSpec((tm, HALF), lambda i: (i, 0)),
            pl.BlockSpec((TOP_K, tm), lambda i: (0, i)),
            pl.BlockSpec((TOP_K, tm), lambda i: (0, i)),
            pl.BlockSpec((TOP_K, tm), lambda i: (0, i)),
            pl.BlockSpec((1, N_EXPERTS, 128), lambda i: (i, 0, 0)),
        ],
        out_shape=[
            jax.ShapeDtypeStruct((T, HALF), jnp.uint32),
            jax.ShapeDtypeStruct((TOP_K, T), jnp.int32),
            jax.ShapeDtypeStruct((TOP_K, T), jnp.float32),
            jax.ShapeDtypeStruct((TOP_K, T), jnp.int32),
            jax.ShapeDtypeStruct((T // tm, N_EXPERTS, 128), jnp.int32),
        ],
        compiler_params=_cparams(("arbitrary",)),
        name="ffn_route",
    )(h, g, wr_t, br)


def _unpack_rows(xw):
    lo = pltpu.bitcast(jnp.left_shift(xw, jnp.uint32(16)), jnp.float32).astype(jnp.bfloat16)
    hi = pltpu.bitcast(xw & jnp.uint32(0xFFFF0000), jnp.float32).astype(jnp.bfloat16)
    return lo, hi


def _experts_kernel(ue_ref, ub_ref, un_ref, meta_ref, tok_ref,
                    bgu_ref, bd_ref, hn_hbm, wgu_hbm, wd_hbm, y_hbm,
                    xbuf, act, wstage, wbf, ystage, gsem, wsem, ysem):
    u = pl.program_id(0)
    n_units = meta_ref[0]
    n_used_blocks = meta_ref[1]
    dot = functools.partial(jnp.dot, preferred_element_type=jnp.float32)

    def gather_copy(unit, r):
        tok = tok_ref[ub_ref[unit] * EB + r]
        return pltpu.make_async_copy(hn_hbm.at[pl.ds(tok, 1), :], xbuf.at[pl.ds(r, 1), :], gsem)

    def start_gather(unit):
        def body(q, carry):
            for d in range(GATHER_UNROLL):
                gather_copy(unit, q * GATHER_UNROLL + d).start()
            return carry
        lax.fori_loop(0, UNIT_ROWS // GATHER_UNROLL, body, 0)

    def start_gather_part(unit, part):
        for d in range(GATHER_PER_TILE):
            gather_copy(unit, part * GATHER_PER_TILE + d).start()

    def wait_gather():
        pltpu.make_async_copy(hn_hbm.at[pl.ds(0, UNIT_ROWS), :], xbuf, gsem).wait()

    def tile_copies(w_hbm, e, cols, slot):
        out = []
        chunk = D_MODEL // W_ROW_SPLIT
        for part, c0 in enumerate(cols):
            for kh in range(W_ROW_SPLIT):
                rows = pl.ds(kh * chunk, chunk)
                out.append(pltpu.make_async_copy(w_hbm.at[e, rows, pl.ds(c0, TW)],
                                                 wstage.at[slot, part, rows, :], wsem.at[slot]))
        return out

    def gate_up_copies(e, j, slot):
        c0 = pl.multiple_of(j * TW, TW)
        return tile_copies(wgu_hbm, e, (c0, D_FF + c0), slot)

    def down_copies(e, j, slot):
        c0 = pl.multiple_of(j * 2 * TW, 2 * TW)
        return tile_copies(wd_hbm, e, (c0, c0 + TW), slot)

    def start(copies):
        for c in copies:
            c.start()

    def wait_tile(slot):
        pltpu.make_async_copy(wgu_hbm.at[pl.ds(0, 2), :, pl.ds(0, TW)], wstage.at[slot], wsem.at[slot]).wait()

    def cast_tile(slot):
        wbf[:, :TW] = wstage[slot, 0].astype(jnp.bfloat16)
        wbf[:, TW:] = wstage[slot, 1].astype(jnp.bfloat16)

    def y_copy(block, j, ys):
        r0 = pl.multiple_of(block * EB, EB)
        c0 = pl.multiple_of(j * 2 * TW, 2 * TW)
        return pltpu.make_async_copy(ystage.at[ys], y_hbm.at[pl.ds(r0, UNIT_ROWS), pl.ds(c0, 2 * TW)],
                                     ysem.at[ys])

    def zero_copy(block, j):
        r0 = pl.multiple_of(block * EB, EB)
        c0 = pl.multiple_of(j * 2 * TW, 2 * TW)
        return pltpu.make_async_copy(ystage.at[0, pl.ds(0, EB), :],
                                     y_hbm.at[pl.ds(r0, EB), pl.ds(c0, 2 * TW)], ysem.at[0])

    def gate_up_rows(r0, j, bg, bl, m=EB):
        lo, hi = _unpack_rows(xbuf[pl.ds(r0, m), :])
        hcat = dot(lo, wbf[:HALF, :]) + dot(hi, wbf[HALF:, :])
        glu = jnp.minimum(hcat[:, :TW] + bg, SWIGLU_LIMIT)
        lin = jnp.clip(hcat[:, TW:] + bl, -SWIGLU_LIMIT, SWIGLU_LIMIT)
        a = glu * jax.nn.sigmoid(SWIGLU_ALPHA * glu) * (lin + 1.0)
        act[j, pl.ds(r0, m), :] = a.astype(act.dtype)

    def down_rows(r0, ys, bd, m=EB):
        a = jnp.concatenate([act[jj, pl.ds(r0, m), :] for jj in range(NT1)], axis=1)
        ystage[ys, pl.ds(r0, m), :] = dot(a, wbf[...]) + bd

    def for_blocks_after_first(nb, block_fn):
        rest = nb - 1
        for run in (4, 2, 1):
            @pl.when((rest & run) != 0)
            def _(run=run):
                first = 1 + (rest & ~(2 * run - 1))
                step = min(run, 2)
                for q in range(0, run, step):
                    block_fn(pl.multiple_of((first + q) * EB, EB), step * EB)

    @pl.when(u == 0)
    def _prologue():
        ystage[...] = jnp.zeros(ystage.shape, ystage.dtype)
        start_gather(0)
        start(gate_up_copies(ue_ref[0], 0, 0))

    @pl.when(u < n_units)
    def _unit():
        e = ue_ref[u]
        b0 = ub_ref[u]
        nb = un_ref[u]
        has_next = u + 1 < n_units
        wait_gather()

        def gate_up_tile(j, carry):
            slot = j % 2
            wait_tile(slot)

            @pl.when(j + 1 < NT1)
            def _():
                start(gate_up_copies(e, j + 1, 1 - slot))

            @pl.when(j + 1 == NT1)
            def _():
                start(down_copies(e, 0, 1 - slot))

            bg = bgu_ref[0, pl.ds(j, 1), :]
            bl = bgu_ref[0, pl.ds(NT1 + j, 1), :]
            cast_tile(slot)
            gate_up_rows(0, j, bg, bl)
            for_blocks_after_first(nb, lambda r0, m: gate_up_rows(r0, j, bg, bl, m))
            return carry

        lax.fori_loop(0, NT1, gate_up_tile, 0)

        def down_tile(j, carry):
            slot = j % 2
            wait_tile(slot)

            @pl.when(j + 1 < NT2)
            def _():
                start(down_copies(e, j + 1, 1 - slot))

            @pl.when((j + 1 == NT2) & has_next)
            def _():
                start(gate_up_copies(ue_ref[u + 1], 0, 1 - slot))

            bd = bd_ref[0, pl.ds(j, 1), :]
            ys = j % 2

            @pl.when(j >= 2)
            def _():
                y_copy(b0, j, ys).wait()

            cast_tile(slot)
            start_gather_part(u + 1, j)
            down_rows(0, ys, bd)
            for_blocks_after_first(nb, lambda r0, m: down_rows(r0, ys, bd, m))
            y_copy(b0, j, ys).start()
            return carry

        lax.fori_loop(0, NT2, down_tile, 0)
        y_copy(b0, 0, 0).wait()
        y_copy(b0, 0, 1).wait()

        @pl.when(jnp.logical_not(has_next))
        def _():
            wait_gather()

    @pl.when(u == N_UNITS - 1)
    def _zero_unused_rows():
        ystage[0] = jnp.zeros(ystage.shape[1:], ystage.dtype)

        def body(b, carry):
            for j in range(NT2):
                zero_copy(b, j).start()
            for j in range(NT2):
                zero_copy(b, j).wait()
            return carry

        lax.fori_loop(n_used_blocks, N_BLOCKS + UNIT_BLOCKS, body, 0)


def _experts(unit_tables, row_tok, hn_packed, w_gate_up, b_gate_up, w_down, b_down):
    any_spec = pl.BlockSpec(memory_space=pl.ANY)
    return pl.pallas_call(
        _experts_kernel,
        grid_spec=pltpu.PrefetchScalarGridSpec(
            num_scalar_prefetch=5,
            grid=(N_UNITS,),
            in_specs=[
                pl.BlockSpec((1, 2 * NT1, TW), lambda u, ue, ub, un, meta, tok: (ue[u], 0, 0)),
                pl.BlockSpec((1, NT2, 2 * TW), lambda u, ue, ub, un, meta, tok: (ue[u], 0, 0)),
                any_spec, any_spec, any_spec,
            ],
            out_specs=any_spec,
            scratch_shapes=[
                pltpu.VMEM((UNIT_ROWS, HALF), jnp.uint32),
                pltpu.VMEM((NT1, UNIT_ROWS, TW), jnp.bfloat16),
                pltpu.VMEM((2, 2, D_MODEL, TW), jnp.float32),
                pltpu.VMEM((D_MODEL, 2 * TW), jnp.bfloat16),
                pltpu.VMEM((2, UNIT_ROWS, 2 * TW), jnp.float32),
                pltpu.SemaphoreType.DMA(()),
                pltpu.SemaphoreType.DMA((2,)),
                pltpu.SemaphoreType.DMA((2,)),
            ],
        ),
        out_shape=jax.ShapeDtypeStruct((N_ROWS + UNIT_ROWS, D_MODEL), jnp.float32),
        compiler_params=_cparams(("arbitrary",)),
        name="ffn_experts",
    )(*unit_tables, row_tok, b_gate_up, b_down, hn_packed, w_gate_up, w_down)


def _combine_kernel(pos_ref, h_ref, gate_ref, g_ref, y_hbm, o_ref, rows, sem, *, tok0):
    i = pl.program_id(0)
    slot = i % 2

    def start_rows(step, dst_slot):
        base = tok0 + step * COMB_TOK

        def body(t, carry):
            for k in range(TOP_K):
                p = pos_ref[k * T + base + t]
                pltpu.make_async_copy(y_hbm.at[pl.ds(p, 1), :], rows.at[dst_slot, k, pl.ds(t, 1), :],
                                      sem.at[dst_slot]).start()
            return carry

        lax.fori_loop(0, COMB_TOK, body, 0)

    @pl.when(i == 0)
    def _():
        start_rows(0, 0)

    @pl.when(i + 1 < pl.num_programs(0))
    def _():
        start_rows(i + 1, 1 - slot)

    for k in range(TOP_K):
        pltpu.make_async_copy(y_hbm.at[pl.ds(0, COMB_TOK), :], rows.at[slot, k], sem.at[slot]).wait()
    gate = gate_ref[...]
    acc = rows[slot, 0] * gate[:, 0:1]
    for k in range(1, TOP_K):
        acc = acc + rows[slot, k] * gate[:, k:k + 1]
    x = h_ref[...] + acc
    y = x * lax.rsqrt(jnp.mean(x * x, axis=-1, keepdims=True) + EPS)
    o_ref[...] = y * g_ref[...]


def _combine(pos_flat, h, gate_tk, g_final, y_sorted, tok0, n_tok):
    blk0 = tok0 // COMB_TOK
    return pl.pallas_call(
        functools.partial(_combine_kernel, tok0=tok0),
        grid_spec=pltpu.PrefetchScalarGridSpec(
            num_scalar_prefetch=1,
            grid=(n_tok // COMB_TOK,),
            in_specs=[
                pl.BlockSpec((COMB_TOK, D_MODEL), lambda i, pos: (blk0 + i, 0)),
                pl.BlockSpec((COMB_TOK, TOP_K), lambda i, pos: (blk0 + i, 0)),
                pl.BlockSpec((1, D_MODEL), lambda i, pos: (0, 0)),
                pl.BlockSpec(memory_space=pl.ANY),
            ],
            out_specs=pl.BlockSpec((COMB_TOK, D_MODEL), lambda i, pos: (i, 0)),
            scratch_shapes=[
                pltpu.VMEM((2, TOP_K, COMB_TOK, D_MODEL), jnp.float32),
                pltpu.SemaphoreType.DMA((2,)),
            ],
        ),
        out_shape=jax.ShapeDtypeStruct((n_tok, D_MODEL), jnp.float32),
        compiler_params=_cparams(("arbitrary",)),
        name="ffn_combine",
    )(pos_flat, h, gate_tk, g_final, y_sorted)


def _routing(idx_t, rank_t, hist):
    i32 = jnp.int32
    tile_off = jnp.cumsum(hist, axis=0) - hist
    counts = jnp.sum(hist, axis=0)
    nblk = (counts + EB - 1) // EB
    blk_end = jnp.cumsum(nblk)
    blk_start = blk_end - nblk
    base_tab = tile_off + blk_start[None, :] * EB
    n_tiles = T // ROUTE_TILE
    onehot = idx_t.reshape(TOP_K, n_tiles, ROUTE_TILE, 1) == jnp.arange(N_EXPERTS, dtype=i32)
    base = jnp.sum(jnp.where(onehot, base_tab[None, :, None, :], 0), axis=-1)
    dest = (base.reshape(TOP_K, T) + rank_t).reshape(-1).astype(i32)
    tok = jnp.tile(jnp.arange(T, dtype=i32), TOP_K)
    row_tok = jnp.zeros((N_ROWS + UNIT_ROWS,), i32).at[dest].set(tok, unique_indices=True)

    n_unit_e = (nblk + UNIT_BLOCKS - 1) // UNIT_BLOCKS
    unit_end = jnp.cumsum(n_unit_e)
    n_units = unit_end[-1]
    u = jnp.minimum(jnp.arange(N_UNITS, dtype=i32), n_units - 1)
    e = jnp.minimum(jnp.sum(unit_end[None, :] <= u[:, None], axis=1), N_EXPERTS - 1).astype(i32)
    local = u - (unit_end[e] - n_unit_e[e])
    b0 = blk_start[e] + local * UNIT_BLOCKS
    nb = jnp.minimum(nblk[e] - local * UNIT_BLOCKS, UNIT_BLOCKS)
    meta = jnp.stack([n_units, blk_end[-1]]).astype(i32)
    return row_tok, dest, (e, b0.astype(i32), nb.astype(i32), meta)


def kernel(x_prompt, x_sample, cache_conv, cache_pool, norm_mix_g, w_in, conv_k, w_conv_out, w_pool_map,
           pool_scale, w_o, norm_ffn_g, w_router, b_router, w_gate_up, b_gate_up, w_down, b_down,
           norm_final_g):
    bf = jnp.bfloat16
    xp = x_prompt.reshape(T_P, D_MODEL)
    xs = x_sample.reshape(T_S, D_MODEL)
    cc_pad = jnp.pad(cache_conv[0], ((0, 0), (CONV_PAD - CONV_HIST, 0), (0, 0)))
    cp_pad = jnp.pad(cache_pool[0], ((0, 0), (POOL_PAD - POOL_HIST, 0), (0, 0)))

    xn = _norm(xp, xs, norm_mix_g)
    w_in_bf = w_in[0].astype(bf)
    z, pooled, cs_p, ps_p, cs_s, ps_s = _proj(xn, w_in_bf, conv_k[0], cc_pad, cp_pad)
    sg = _gates(xn, w_in_bf)
    mix = _merge(z, w_conv_out[0].astype(bf), pooled, w_pool_map[0].astype(bf), pool_scale, sg)
    h = _oproj(mix, w_o[0].astype(bf), xp, xs)

    hn_packed, idx_t, gate_t, rank_t, hist = _route(h, norm_ffn_g, w_router[0].T,
                                                    b_router[0].reshape(N_EXPERTS, 1))
    row_tok, dest, unit_tables = _routing(idx_t, rank_t, hist[:, :, 0])
    y_sorted = _experts(unit_tables, row_tok, hn_packed,
                        w_gate_up[0], b_gate_up[0].reshape(N_EXPERTS, 2 * NT1, TW),
                        w_down[0], b_down[0].reshape(N_EXPERTS, NT2, 2 * TW))

    gate_tk = gate_t.T
    g_final = norm_final_g.reshape(1, D_MODEL)
    y_p = _combine(dest, h, gate_tk, g_final, y_sorted, 0, T_P)
    y_s = _combine(dest, h, gate_tk, g_final, y_sorted, T_P, T_S)
    return (y_p.reshape(BATCH, SEQ, D_MODEL), y_s.reshape(DEC_BATCH, DEC_SEQ, D_MODEL),
            cs_p[None], ps_p[None], cs_s[None], ps_s[None])
```

```python
import functools

import jax
import jax.numpy as jnp
from jax import lax
from jax.experimental import pallas as pl
from jax.experimental.pallas import tpu as pltpu

D_MODEL = 4096
BATCH = 4
SEQ = 2048
DEC_BATCH = 16
DEC_SEQ = 64
D_CONV = D_MODEL // 2
D_POOL = D_MODEL // 2
CONV_HIST = 2
POOL_HIST = 15
POOL_GROUP_DIM = D_POOL // 4
N_EXPERTS = 32
TOP_K = 4
D_FF = D_MODEL
SWIGLU_ALPHA = 1.702
SWIGLU_LIMIT = 7.0
EPS = 1e-5

T_P = BATCH * SEQ
T_S = DEC_BATCH * DEC_SEQ
T = T_P + T_S
HALF = D_MODEL // 2

ROW_TILE = 1024
N_ROW_TILES = T // ROW_TILE
N_PROMPT_TILES = T_P // ROW_TILE
TC = 256
CONV_PAD = 8
POOL_PAD = 16

EB = 256
N_SLOTS = T * TOP_K
N_BLOCKS = N_SLOTS // EB + N_EXPERTS
N_ROWS = N_BLOCKS * EB
UNIT_BLOCKS = 6
UNIT_ROWS = UNIT_BLOCKS * EB
N_UNITS = N_EXPERTS + N_BLOCKS // UNIT_BLOCKS
TW = 256
NT1 = D_FF // TW
NT2 = D_MODEL // (2 * TW)
W_ROW_SPLIT = 8
GATHER_UNROLL = 4
GATHER_PER_TILE = UNIT_ROWS // NT2
ROUTE_TILE = 512
COMB_TOK = 128

VMEM_LIMIT = 56 * 1024 * 1024


def _cparams(sem):
    return pltpu.CompilerParams(dimension_semantics=sem, vmem_limit_bytes=VMEM_LIMIT)


def _norm_kernel(xp_ref, xs_ref, g_ref, o_ref):
    i = pl.program_id(0)

    def body(x):
        y = x * lax.rsqrt(jnp.mean(x * x, axis=-1, keepdims=True) + EPS)
        o_ref[...] = (y * g_ref[...]).astype(o_ref.dtype)

    @pl.when(i < T_P // 512)
    def _():
        body(xp_ref[...])

    @pl.when(i >= T_P // 512)
    def _():
        body(xs_ref[...])


def _norm(xp, xs, g):
    np_ = T_P // 512
    return pl.pallas_call(
        _norm_kernel,
        grid=(T // 512,),
        in_specs=[
            pl.BlockSpec((512, D_MODEL), lambda i: (jnp.minimum(i, np_ - 1), 0)),
            pl.BlockSpec((512, D_MODEL), lambda i: (jnp.maximum(i - np_, 0), 0)),
            pl.BlockSpec((1, D_MODEL), lambda i: (0, 0)),
        ],
        out_specs=pl.BlockSpec((512, D_MODEL), lambda i: (i, 0)),
        out_shape=jax.ShapeDtypeStruct((T, D_MODEL), jnp.bfloat16),
        compiler_params=_cparams(("arbitrary",)),
        name="mix_norm",
    )(xp, xs, g)


def _conv3(ev, ck):
    return ck[2:3, :] * ev + ck[1:2, :] * pltpu.roll(ev, 1, 0) + ck[0:1, :] * pltpu.roll(ev, 2, 0)


def _window_sum(eu, group):
    s2 = eu + pltpu.roll(eu, 1, 0)
    s4 = s2 + pltpu.roll(s2, 2, 0)
    s8 = s4 + pltpu.roll(s4, 4, 0)
    s16 = s8 + pltpu.roll(s8, 8, 0)
    return jnp.where(group == 0, s2, jnp.where(group == 1, s4, jnp.where(group == 2, s8, s16)))


def _proj_kernel(xn_ref, wb_ref, wc_ref, wx_ref, wu_ref, ck_ref, cc_ref, cp_ref,
                 z_ref, pooled_ref, csp_ref, psp_ref, css_ref, pss_ref,
                 vcarry, ucarry, vs, us):
    c = pl.program_id(0)
    i = pl.program_id(1)
    xn = xn_ref[...]
    dot = functools.partial(jnp.dot, preferred_element_type=jnp.float32)
    gate_b = dot(xn, wb_ref[...])
    v = dot(xn, wc_ref[...]) * dot(xn, wx_ref[...])
    u = dot(xn, wu_ref[...])
    ck = ck_ref[...]
    group = c // (POOL_GROUP_DIM // TC)
    window = jnp.left_shift(2, group)

    @pl.when(i < N_PROMPT_TILES)
    def _prompt():
        first = (i % 2) == 0
        vh = jnp.where(first, 0.0, vcarry[...])
        uh = jnp.where(first, 0.0, ucarry[...])
        conv = _conv3(jnp.concatenate([vh, v], axis=0), ck)[CONV_PAD:]
        z_ref[...] = (gate_b * conv).astype(z_ref.dtype)
        win = _window_sum(jnp.concatenate([uh, u], axis=0), group)[POOL_PAD:]
        pos = (i % 2) * ROW_TILE + lax.broadcasted_iota(jnp.int32, (ROW_TILE, 1), 0)
        cnt = jnp.minimum(pos + 1, window).astype(jnp.float32)
        pooled_ref[...] = (win * (1.0 / cnt) - u).astype(pooled_ref.dtype)
        vcarry[...] = v[ROW_TILE - CONV_PAD:]
        ucarry[...] = u[ROW_TILE - POOL_PAD:]
        csp_ref[0] = vcarry[CONV_PAD - CONV_HIST:, :]
        psp_ref[0] = ucarry[POOL_PAD - POOL_HIST:, :]

    @pl.when(i >= N_PROMPT_TILES)
    def _sample():
        ev, eu = [], []
        for s in range(DEC_BATCH):
            ev += [cc_ref[s], v[s * DEC_SEQ:(s + 1) * DEC_SEQ]]
            eu += [cp_ref[s], u[s * DEC_SEQ:(s + 1) * DEC_SEQ]]
        conv_e = _conv3(jnp.concatenate(ev, axis=0), ck)
        win_e = _window_sum(jnp.concatenate(eu, axis=0), group)
        lv, lu = CONV_PAD + DEC_SEQ, POOL_PAD + DEC_SEQ
        conv = jnp.concatenate([conv_e[s * lv + CONV_PAD:(s + 1) * lv] for s in range(DEC_BATCH)], axis=0)
        win = jnp.concatenate([win_e[s * lu + POOL_PAD:(s + 1) * lu] for s in range(DEC_BATCH)], axis=0)
        z_ref[...] = (gate_b * conv).astype(z_ref.dtype)
        inv = 1.0 / window.astype(jnp.float32)
        pooled_ref[...] = (win * inv - u).astype(pooled_ref.dtype)
        vs[...] = v
        us[...] = u
        for s in range(DEC_BATCH):
            end = (s + 1) * DEC_SEQ
            css_ref[s] = vs[end - CONV_HIST:end, :]
            pss_ref[s] = us[end - POOL_HIST:end, :]


def _proj(xn, w_in_bf, conv_k, cache_conv_pad, cache_pool_pad):
    nsec = D_CONV // TC
    pidx = lambda c, i: (jnp.minimum(i, N_PROMPT_TILES - 1) // 2, 0, c)
    return pl.pallas_call(
        _proj_kernel,
        grid=(nsec, N_ROW_TILES),
        in_specs=[
            pl.BlockSpec((ROW_TILE, D_MODEL), lambda c, i: (i, 0)),
            pl.BlockSpec((D_MODEL, TC), lambda c, i: (0, c)),
            pl.BlockSpec((D_MODEL, TC), lambda c, i: (0, nsec + c)),
            pl.BlockSpec((D_MODEL, TC), lambda c, i: (0, 2 * nsec + c)),
            pl.BlockSpec((D_MODEL, TC), lambda c, i: (0, 3 * nsec + c)),
            pl.BlockSpec((3, TC), lambda c, i: (0, c)),
            pl.BlockSpec((DEC_BATCH, CONV_PAD, TC), lambda c, i: (0, 0, c)),
            pl.BlockSpec((DEC_BATCH, POOL_PAD, TC), lambda c, i: (0, 0, c)),
        ],
        out_specs=[
            pl.BlockSpec((ROW_TILE, TC), lambda c, i: (i, c)),
            pl.BlockSpec((ROW_TILE, TC), lambda c, i: (i, c)),
            pl.BlockSpec((1, CONV_HIST, TC), pidx),
            pl.BlockSpec((1, POOL_HIST, TC), pidx),
            pl.BlockSpec((DEC_BATCH, CONV_HIST, TC), lambda c, i: (0, 0, c)),
            pl.BlockSpec((DEC_BATCH, POOL_HIST, TC), lambda c, i: (0, 0, c)),
        ],
        out_shape=[
            jax.ShapeDtypeStruct((T, D_CONV), jnp.bfloat16),
            jax.ShapeDtypeStruct((T, D_POOL), jnp.bfloat16),
            jax.ShapeDtypeStruct((BATCH, CONV_HIST, D_CONV), jnp.float32),
            jax.ShapeDtypeStruct((BATCH, POOL_HIST, D_POOL), jnp.float32),
            jax.ShapeDtypeStruct((DEC_BATCH, CONV_HIST, D_CONV), jnp.float32),
            jax.ShapeDtypeStruct((DEC_BATCH, POOL_HIST, D_POOL), jnp.float32),
        ],
        scratch_shapes=[
            pltpu.VMEM((CONV_PAD, TC), jnp.float32),
            pltpu.VMEM((POOL_PAD, TC), jnp.float32),
            pltpu.VMEM((ROW_TILE, TC), jnp.float32),
            pltpu.VMEM((ROW_TILE, TC), jnp.float32),
        ],
        compiler_params=_cparams(("arbitrary", "arbitrary")),
        name="mix_proj",
    )(xn, w_in_bf, w_in_bf, w_in_bf, w_in_bf, conv_k, cache_conv_pad, cache_pool_pad)


def _gates_kernel(xn_ref, w_ref, o_ref):
    g = jnp.dot(xn_ref[...], w_ref[...], preferred_element_type=jnp.float32)
    o_ref[...] = jax.nn.sigmoid(g).astype(o_ref.dtype)


def _gates(xn, w_in_bf):
    tn = 1024
    off = (3 * D_CONV + D_POOL) // tn
    return pl.pallas_call(
        _gates_kernel,
        grid=(2 * D_MODEL // tn, N_ROW_TILES),
        in_specs=[
            pl.BlockSpec((ROW_TILE, D_MODEL), lambda j, i: (i, 0)),
            pl.BlockSpec((D_MODEL, tn), lambda j, i: (0, off + j)),
        ],
        out_specs=pl.BlockSpec((ROW_TILE, tn), lambda j, i: (i, j)),
        out_shape=jax.ShapeDtypeStruct((T, 2 * D_MODEL), jnp.bfloat16),
        compiler_params=_cparams(("arbitrary", "arbitrary")),
        name="mix_gates",
    )(xn, w_in_bf)


def _merge_kernel(z_ref, wco_ref, p_ref, wpm_ref, scale_ref, ga_ref, gb_ref, o_ref):
    y_conv = jnp.dot(z_ref[...], wco_ref[...], preferred_element_type=jnp.float32)
    y_pool = jnp.dot(p_ref[...], wpm_ref[0], preferred_element_type=jnp.float32) * scale_ref[...]
    mix = ga_ref[...].astype(jnp.float32) * y_conv + gb_ref[...].astype(jnp.float32) * y_pool
    o_ref[...] = mix.astype(o_ref.dtype)


def _merge(z, wco_bf, pooled, wpm_bf, pool_scale, sg):
    tn = D_MODEL // 4
    return pl.pallas_call(
        _merge_kernel,
        grid=(4, N_ROW_TILES),
        in_specs=[
            pl.BlockSpec((ROW_TILE, D_CONV), lambda j, i: (i, 0)),
            pl.BlockSpec((D_CONV, tn), lambda j, i: (0, j)),
            pl.BlockSpec((ROW_TILE, POOL_GROUP_DIM), lambda j, i: (i, j)),
            pl.BlockSpec((1, POOL_GROUP_DIM, tn), lambda j, i: (j, 0, 0)),
            pl.BlockSpec((1, tn), lambda j, i: (0, j)),
            pl.BlockSpec((ROW_TILE, tn), lambda j, i: (i, j)),
            pl.BlockSpec((ROW_TILE, tn), lambda j, i: (i, 4 + j)),
        ],
        out_specs=pl.BlockSpec((ROW_TILE, tn), lambda j, i: (i, j)),
        out_shape=jax.ShapeDtypeStruct((T, D_MODEL), jnp.bfloat16),
        compiler_params=_cparams(("arbitrary", "arbitrary")),
        name="mix_merge",
    )(z, wco_bf, pooled, wpm_bf, pool_scale, sg, sg)


def _oproj_kernel(m_ref, w_ref, xp_ref, xs_ref, o_ref):
    i = pl.program_id(1)
    a = jnp.dot(m_ref[...], w_ref[...], preferred_element_type=jnp.float32)

    @pl.when(i < T_P // 512)
    def _():
        o_ref[...] = xp_ref[...] + a

    @pl.when(i >= T_P // 512)
    def _():
        o_ref[...] = xs_ref[...] + a


def _oproj(mix, w_o_bf, xp, xs):
    tn = 1024
    np_ = T_P // 512
    return pl.pallas_call(
        _oproj_kernel,
        grid=(D_MODEL // tn, T // 512),
        in_specs=[
            pl.BlockSpec((512, D_MODEL), lambda j, i: (i, 0)),
            pl.BlockSpec((D_MODEL, tn), lambda j, i: (0, j)),
            pl.BlockSpec((512, tn), lambda j, i: (jnp.minimum(i, np_ - 1), j)),
            pl.BlockSpec((512, tn), lambda j, i: (jnp.maximum(i - np_, 0), j)),
        ],
        out_specs=pl.BlockSpec((512, tn), lambda j, i: (i, j)),
        out_shape=jax.ShapeDtypeStruct((T, D_MODEL), jnp.float32),
        compiler_params=_cparams(("arbitrary", "arbitrary")),
        name="mix_oproj",
    )(mix, w_o_bf, xp, xs)


def _pack_bf16_pair(lo, hi):
    lo_bits = pltpu.bitcast(lo.astype(jnp.bfloat16).astype(jnp.float32), jnp.uint32)
    hi_bits = pltpu.bitcast(hi.astype(jnp.bfloat16).astype(jnp.float32), jnp.uint32)
    return jnp.right_shift(lo_bits, jnp.uint32(16)) | (hi_bits & jnp.uint32(0xFFFF0000))


def _unpack_f32_pair(w):
    lo = pltpu.bitcast(jnp.left_shift(w, jnp.uint32(16)), jnp.float32)
    hi = pltpu.bitcast(w & jnp.uint32(0xFFFF0000), jnp.float32)
    return lo, hi


def _route_kernel(h_ref, g_ref, wr_ref, br_ref, hn_ref, idx_ref, gate_ref, rank_ref, hist_ref):
    h = h_ref[...]
    hn = h * lax.rsqrt(jnp.mean(h * h, axis=-1, keepdims=True) + EPS) * g_ref[...]
    hn_ref[...] = _pack_bf16_pair(hn[:, :HALF], hn[:, HALF:])
    logits = lax.dot_general(wr_ref[...], hn, (((1,), (1,)), ((), ())),
                             precision=lax.Precision.HIGHEST,
                             preferred_element_type=jnp.float32) + br_ref[...]
    eid = lax.broadcasted_iota(jnp.int32, logits.shape, 0)
    vals, ids = [], []
    for _ in range(TOP_K):
        m = jnp.max(logits, axis=0, keepdims=True)
        sel = jnp.min(jnp.where(logits == m, eid, N_EXPERTS), axis=0, keepdims=True)
        vals.append(m)
        ids.append(sel)
        logits = jnp.where(eid == sel, -jnp.inf, logits)
    ex = [jnp.exp(v - vals[0]) for v in vals]
    den = ex[0] + ex[1] + ex[2] + ex[3]
    for k in range(TOP_K):
        idx_ref[k:k + 1, :] = ids[k]
        gate_ref[k:k + 1, :] = ex[k] / den
    tm = logits.shape[1]
    tri = (lax.broadcasted_iota(jnp.int32, (tm, tm), 0)
           <= lax.broadcasted_iota(jnp.int32, (tm, tm), 1)).astype(jnp.bfloat16)
    run = jnp.zeros((N_EXPERTS, 1), jnp.float32)
    for k in range(TOP_K):
        oh = (eid == ids[k]).astype(jnp.float32)
        seen = jnp.dot(oh.astype(jnp.bfloat16), tri, preferred_element_type=jnp.float32)
        rank = jnp.sum(oh * (seen - 1.0 + run), axis=0, keepdims=True)
        rank_ref[k:k + 1, :] = rank.astype(jnp.int32)
        run = run + jnp.sum(oh, axis=1, keepdims=True)
    hist_ref[0] = jnp.broadcast_to(run, (N_EXPERTS, 128)).astype(jnp.int32)


def _route(h, g, wr_t, br):
    tm = ROUTE_TILE
    return pl.pallas_call(
        _route_kernel,
        grid=(T // tm,),
        in_specs=[
            pl.BlockSpec((tm, D_MODEL), lambda i: (i, 0)),
            pl.BlockSpec((1, D_MODEL), lambda i: (0, 0)),
            pl.BlockSpec((N_EXPERTS, D_MODEL), lambda i: (0, 0)),
            pl.BlockSpec((N_EXPERTS, 1), lambda i: (0, 0)),
        ],
        out_specs=[
            pl.BlockSpec((tm, HALF), lambda i: (i, 0)),
            pl.BlockSpec((TOP_K, tm), lambda i: (0, i)),
            pl.BlockSpec((TOP_K, tm), lambda i: (0, i)),
            pl.BlockSpec((TOP_K, tm), lambda i: (0, i)),
            pl.BlockSpec((1, N_EXPERTS, 128), lambda i: (i, 0, 0)),
        ],
        out_shape=[
            jax.ShapeDtypeStruct((T, HALF), jnp.uint32),
            jax.ShapeDtypeStruct((TOP_K, T), jnp.int32),
            jax.ShapeDtypeStruct((TOP_K, T), jnp.float32),
            jax.ShapeDtypeStruct((TOP_K, T), jnp.int32),
            jax.ShapeDtypeStruct((T // tm, N_EXPERTS, 128), jnp.int32),
        ],
        compiler_params=_cparams(("arbitrary",)),
        name="ffn_route",
    )(h, g, wr_t, br)


def _unpack_rows(xw):
    lo, hi = _unpack_f32_pair(xw)
    return lo.astype(jnp.bfloat16), hi.astype(jnp.bfloat16)


def _experts_kernel(ue_ref, ub_ref, un_ref, meta_ref, tok_ref,
                    bgu_ref, bd_ref, hn_hbm, wgu_hbm, wd_hbm, y_hbm,
                    xbuf, act, wstage, wbf, ystage, gsem, wsem, ysem):
    u = pl.program_id(0)
    n_units = meta_ref[0]
    n_used_blocks = meta_ref[1]
    dot = functools.partial(jnp.dot, preferred_element_type=jnp.float32)

    def gather_copy(unit, r):
        tok = tok_ref[ub_ref[unit] * EB + r]
        return pltpu.make_async_copy(hn_hbm.at[pl.ds(tok, 1), :], xbuf.at[pl.ds(r, 1), :], gsem)

    def start_gather(unit):
        def body(q, carry):
            for d in range(GATHER_UNROLL):
                gather_copy(unit, q * GATHER_UNROLL + d).start()
            return carry
        lax.fori_loop(0, UNIT_ROWS // GATHER_UNROLL, body, 0)

    def start_gather_part(unit, part):
        for d in range(GATHER_PER_TILE):
            gather_copy(unit, part * GATHER_PER_TILE + d).start()

    def wait_gather():
        pltpu.make_async_copy(hn_hbm.at[pl.ds(0, UNIT_ROWS), :], xbuf, gsem).wait()

    def tile_copies(w_hbm, e, cols, slot):
        out = []
        chunk = D_MODEL // W_ROW_SPLIT
        for part, c0 in enumerate(cols):
            for kh in range(W_ROW_SPLIT):
                rows = pl.ds(kh * chunk, chunk)
                out.append(pltpu.make_async_copy(w_hbm.at[e, rows, pl.ds(c0, TW)],
                                                 wstage.at[slot, part, rows, :], wsem.at[slot]))
        return out

    def gate_up_copies(e, j, slot):
        c0 = pl.multiple_of(j * TW, TW)
        return tile_copies(wgu_hbm, e, (c0, D_FF + c0), slot)

    def down_copies(e, j, slot):
        c0 = pl.multiple_of(j * 2 * TW, 2 * TW)
        return tile_copies(wd_hbm, e, (c0, c0 + TW), slot)

    def start(copies):
        for c in copies:
            c.start()

    def wait_tile(slot):
        pltpu.make_async_copy(wgu_hbm.at[pl.ds(0, 2), :, pl.ds(0, TW)], wstage.at[slot], wsem.at[slot]).wait()

    def cast_tile(slot):
        wbf[:, :TW] = wstage[slot, 0].astype(jnp.bfloat16)
        wbf[:, TW:] = wstage[slot, 1].astype(jnp.bfloat16)

    def y_copy(block, j, ys):
        r0 = pl.multiple_of(block * EB, EB)
        c0 = pl.multiple_of(j * TW, TW)
        return pltpu.make_async_copy(ystage.at[ys], y_hbm.at[pl.ds(r0, UNIT_ROWS), pl.ds(c0, TW)],
                                     ysem.at[ys])

    def zero_copy(block, j):
        r0 = pl.multiple_of(block * EB, EB)
        c0 = pl.multiple_of(j * TW, TW)
        return pltpu.make_async_copy(ystage.at[0, pl.ds(0, EB), :],
                                     y_hbm.at[pl.ds(r0, EB), pl.ds(c0, TW)], ysem.at[0])

    def gate_up_rows(r0, j, bg, bl, m=EB):
        lo, hi = _unpack_rows(xbuf[pl.ds(r0, m), :])
        hcat = dot(jnp.concatenate([lo, hi], axis=1), wbf[...])
        glu = jnp.minimum(hcat[:, :TW] + bg, SWIGLU_LIMIT)
        lin = jnp.clip(hcat[:, TW:] + bl, -SWIGLU_LIMIT, SWIGLU_LIMIT)
        a = glu * jax.nn.sigmoid(SWIGLU_ALPHA * glu) * (lin + 1.0)
        act[j, pl.ds(r0, m), :] = a.astype(act.dtype)

    def down_rows(r0, ys, bd, m=EB):
        a = jnp.concatenate([act[jj, pl.ds(r0, m), :] for jj in range(NT1)], axis=1)
        y = dot(a, wbf[...]) + bd
        ystage[ys, pl.ds(r0, m), :] = _pack_bf16_pair(y[:, :TW], y[:, TW:])

    def for_blocks_after_first(nb, block_fn):
        rest = nb - 1
        for run in (4, 2, 1):
            @pl.when((rest & run) != 0)
            def _(run=run):
                first = 1 + (rest & ~(2 * run - 1))
                for q in range(run):
                    block_fn(pl.multiple_of((first + q) * EB, EB), EB)

    @pl.when(u == 0)
    def _prologue():
        ystage[...] = jnp.zeros(ystage.shape, ystage.dtype)
        start_gather(0)
        start(gate_up_copies(ue_ref[0], 0, 0))

    @pl.when(u < n_units)
    def _unit():
        e = ue_ref[u]
        b0 = ub_ref[u]
        nb = un_ref[u]
        has_next = u + 1 < n_units
        wait_gather()

        def gate_up_tile(j, carry):
            slot = j % 2
            wait_tile(slot)

            @pl.when(j + 1 < NT1)
            def _():
                start(gate_up_copies(e, j + 1, 1 - slot))

            @pl.when(j + 1 == NT1)
            def _():
                start(down_copies(e, 0, 1 - slot))

            bg = bgu_ref[0, pl.ds(j, 1), :]
            bl = bgu_ref[0, pl.ds(NT1 + j, 1), :]
            cast_tile(slot)
            gate_up_rows(0, j, bg, bl)
            for_blocks_after_first(nb, lambda r0, m: gate_up_rows(r0, j, bg, bl, m))
            return carry

        lax.fori_loop(0, NT1, gate_up_tile, 0)

        def down_tile(j, carry):
            slot = j % 2
            wait_tile(slot)

            @pl.when(j + 1 < NT2)
            def _():
                start(down_copies(e, j + 1, 1 - slot))

            @pl.when((j + 1 == NT2) & has_next)
            def _():
                start(gate_up_copies(ue_ref[u + 1], 0, 1 - slot))

            bd = bd_ref[0, pl.ds(j, 1), :]
            ys = j % 2

            @pl.when(j >= 2)
            def _():
                y_copy(b0, j, ys).wait()

            cast_tile(slot)
            start_gather_part(u + 1, j)
            down_rows(0, ys, bd)
            for_blocks_after_first(nb, lambda r0, m: down_rows(r0, ys, bd, m))
            y_copy(b0, j, ys).start()
            return carry

        lax.fori_loop(0, NT2, down_tile, 0)
        y_copy(b0, 0, 0).wait()
        y_copy(b0, 0, 1).wait()

        @pl.when(jnp.logical_not(has_next))
        def _():
            wait_gather()

    @pl.when(u == N_UNITS - 1)
    def _zero_unused_rows():
        ystage[0] = jnp.zeros(ystage.shape[1:], ystage.dtype)

        def body(b, carry):
            for j in range(NT2):
                zero_copy(b, j).start()
            for j in range(NT2):
                zero_copy(b, j).wait()
            return carry

        lax.fori_loop(n_used_blocks, N_BLOCKS + UNIT_BLOCKS, body, 0)


def _experts(unit_tables, row_tok, hn_packed, w_gate_up, b_gate_up, w_down, b_down):
    any_spec = pl.BlockSpec(memory_space=pl.ANY)
    return pl.pallas_call(
        _experts_kernel,
        grid_spec=pltpu.PrefetchScalarGridSpec(
            num_scalar_prefetch=5,
            grid=(N_UNITS,),
            in_specs=[
                pl.BlockSpec((1, 2 * NT1, TW), lambda u, ue, ub, un, meta, tok: (ue[u], 0, 0)),
                pl.BlockSpec((1, NT2, 2 * TW), lambda u, ue, ub, un, meta, tok: (ue[u], 0, 0)),
                any_spec, any_spec, any_spec,
            ],
            out_specs=any_spec,
            scratch_shapes=[
                pltpu.VMEM((UNIT_ROWS, HALF), jnp.uint32),
                pltpu.VMEM((NT1, UNIT_ROWS, TW), jnp.bfloat16),
                pltpu.VMEM((2, 2, D_MODEL, TW), jnp.float32),
                pltpu.VMEM((D_MODEL, 2 * TW), jnp.bfloat16),
                pltpu.VMEM((2, UNIT_ROWS, TW), jnp.uint32),
                pltpu.SemaphoreType.DMA(()),
                pltpu.SemaphoreType.DMA((2,)),
                pltpu.SemaphoreType.DMA((2,)),
            ],
        ),
        out_shape=jax.ShapeDtypeStruct((N_ROWS + UNIT_ROWS, HALF), jnp.uint32),
        compiler_params=_cparams(("arbitrary",)),
        name="ffn_experts",
    )(*unit_tables, row_tok, b_gate_up, b_down, hn_packed, w_gate_up, w_down)


def _combine_kernel(pos_ref, h_ref, gate_ref, g_ref, y_hbm, o_ref, rows, sem, *, tok0):
    i = pl.program_id(0)
    slot = i % 2

    def start_rows(step, dst_slot):
        base = tok0 + step * COMB_TOK

        def body(t, carry):
            for k in range(TOP_K):
                p = pos_ref[k * T + base + t]
                pltpu.make_async_copy(y_hbm.at[pl.ds(p, 1), :], rows.at[dst_slot, k, pl.ds(t, 1), :],
                                      sem.at[dst_slot]).start()
            return carry

        lax.fori_loop(0, COMB_TOK, body, 0)

    @pl.when(i == 0)
    def _():
        start_rows(0, 0)

    @pl.when(i + 1 < pl.num_programs(0))
    def _():
        start_rows(i + 1, 1 - slot)

    for k in range(TOP_K):
        pltpu.make_async_copy(y_hbm.at[pl.ds(0, COMB_TOK), :], rows.at[slot, k], sem.at[slot]).wait()
    gate = gate_ref[...]
    acc_lo = acc_hi = None
    for k in range(TOP_K):
        lo, hi = _unpack_f32_pair(rows[slot, k])
        g = gate[:, k:k + 1]
        acc_lo = lo * g if k == 0 else acc_lo + lo * g
        acc_hi = hi * g if k == 0 else acc_hi + hi * g
    pieces = []
    for j in range(NT2):
        pieces += [acc_lo[:, j * TW:(j + 1) * TW], acc_hi[:, j * TW:(j + 1) * TW]]
    x = h_ref[...] + jnp.concatenate(pieces, axis=1)
    y = x * lax.rsqrt(jnp.mean(x * x, axis=-1, keepdims=True) + EPS)
    o_ref[...] = y * g_ref[...]


def _combine(pos_flat, h, gate_tk, g_final, y_sorted, tok0, n_tok):
    blk0 = tok0 // COMB_TOK
    return pl.pallas_call(
        functools.partial(_combine_kernel, tok0=tok0),
        grid_spec=pltpu.PrefetchScalarGridSpec(
            num_scalar_prefetch=1,
            grid=(n_tok // COMB_TOK,),
            in_specs=[
                pl.BlockSpec((COMB_TOK, D_MODEL), lambda i, pos: (blk0 + i, 0)),
                pl.BlockSpec((COMB_TOK, TOP_K), lambda i, pos: (blk0 + i, 0)),
                pl.BlockSpec((1, D_MODEL), lambda i, pos: (0, 0)),
                pl.BlockSpec(memory_space=pl.ANY),
            ],
            out_specs=pl.BlockSpec((COMB_TOK, D_MODEL), lambda i, pos: (i, 0)),
            scratch_shapes=[
                pltpu.VMEM((2, TOP_K, COMB_TOK, HALF), jnp.uint32),
                pltpu.SemaphoreType.DMA((2,)),
            ],
        ),
        out_shape=jax.ShapeDtypeStruct((n_tok, D_MODEL), jnp.float32),
        compiler_params=_cparams(("arbitrary",)),
        name="ffn_combine",
    )(pos_flat, h, gate_tk, g_final, y_sorted)


def _routing(idx_t, rank_t, hist):
    i32 = jnp.int32
    tile_off = jnp.cumsum(hist, axis=0) - hist
    counts = jnp.sum(hist, axis=0)
    nblk = (counts + EB - 1) // EB
    blk_end = jnp.cumsum(nblk)
    blk_start = blk_end - nblk
    base_tab = tile_off + blk_start[None, :] * EB
    n_tiles = T // ROUTE_TILE
    onehot = idx_t.reshape(TOP_K, n_tiles, ROUTE_TILE, 1) == jnp.arange(N_EXPERTS, dtype=i32)
    base = jnp.sum(jnp.where(onehot, base_tab[None, :, None, :], 0), axis=-1)
    dest = (base.reshape(TOP_K, T) + rank_t).reshape(-1).astype(i32)
    tok = jnp.tile(jnp.arange(T, dtype=i32), TOP_K)
    row_tok = jnp.zeros((N_ROWS + UNIT_ROWS,), i32).at[dest].set(tok, unique_indices=True)

    n_unit_e = (nblk + UNIT_BLOCKS - 1) // UNIT_BLOCKS
    unit_end = jnp.cumsum(n_unit_e)
    n_units = unit_end[-1]
    u = jnp.minimum(jnp.arange(N_UNITS, dtype=i32), n_units - 1)
    e = jnp.minimum(jnp.sum(unit_end[None, :] <= u[:, None], axis=1), N_EXPERTS - 1).astype(i32)
    local = u - (unit_end[e] - n_unit_e[e])
    b0 = blk_start[e] + local * UNIT_BLOCKS
    nb = jnp.minimum(nblk[e] - local * UNIT_BLOCKS, UNIT_BLOCKS)
    meta = jnp.stack([n_units, blk_end[-1]]).astype(i32)
    return row_tok, dest, (e, b0.astype(i32), nb.astype(i32), meta)


def kernel(x_prompt, x_sample, cache_conv, cache_pool, norm_mix_g, w_in, conv_k, w_conv_out, w_pool_map,
           pool_scale, w_o, norm_ffn_g, w_router, b_router, w_gate_up, b_gate_up, w_down, b_down,
           norm_final_g):
    bf = jnp.bfloat16
    xp = x_prompt.reshape(T_P, D_MODEL)
    xs = x_sample.reshape(T_S, D_MODEL)
    cc_pad = jnp.pad(cache_conv[0], ((0, 0), (CONV_PAD - CONV_HIST, 0), (0, 0)))
    cp_pad = jnp.pad(cache_pool[0], ((0, 0), (POOL_PAD - POOL_HIST, 0), (0, 0)))

    xn = _norm(xp, xs, norm_mix_g)
    w_in_bf = w_in[0].astype(bf)
    z, pooled, cs_p, ps_p, cs_s, ps_s = _proj(xn, w_in_bf, conv_k[0], cc_pad, cp_pad)
    sg = _gates(xn, w_in_bf)
    mix = _merge(z, w_conv_out[0].astype(bf), pooled, w_pool_map[0].astype(bf), pool_scale, sg)
    h = _oproj(mix, w_o[0].astype(bf), xp, xs)

    hn_packed, idx_t, gate_t, rank_t, hist = _route(h, norm_ffn_g, w_router[0].T,
                                                    b_router[0].reshape(N_EXPERTS, 1))
    row_tok, dest, unit_tables = _routing(idx_t, rank_t, hist[:, :, 0])
    y_sorted = _experts(unit_tables, row_tok, hn_packed,
                        w_gate_up[0], b_gate_up[0].reshape(N_EXPERTS, 2 * NT1, TW),
                        w_down[0], b_down[0].reshape(N_EXPERTS, NT2, 2 * TW))

    gate_tk = gate_t.T
    g_final = norm_final_g.reshape(1, D_MODEL)
    y_p = _combine(dest, h, gate_tk, g_final, y_sorted, 0, T_P)
    y_s = _combine(dest, h, gate_tk, g_final, y_sorted, T_P, T_S)
    return (y_p.reshape(BATCH, SEQ, D_MODEL), y_s.reshape(DEC_BATCH, DEC_SEQ, D_MODEL),
            cs_p[None], ps_p[None], cs_s[None], ps_s[None])
```

```python
import functools

import jax
import jax.numpy as jnp
from jax import lax
from jax.experimental import pallas as pl
from jax.experimental.pallas import tpu as pltpu

D_MODEL = 4096
BATCH = 4
SEQ = 2048
DEC_BATCH = 16
DEC_SEQ = 64
D_CONV = D_MODEL // 2
D_POOL = D_MODEL // 2
CONV_HIST = 2
POOL_HIST = 15
POOL_GROUP_DIM = D_POOL // 4
N_EXPERTS = 32
TOP_K = 4
D_FF = D_MODEL
SWIGLU_ALPHA = 1.702
SWIGLU_LIMIT = 7.0
EPS = 1e-5

T_P = BATCH * SEQ
T_S = DEC_BATCH * DEC_SEQ
T = T_P + T_S
HALF = D_MODEL // 2

ROW_TILE = 1024
N_ROW_TILES = T // ROW_TILE
N_PROMPT_TILES = T_P // ROW_TILE
TC = 256
CONV_PAD = 8
POOL_PAD = 16

EB = 256
N_SLOTS = T * TOP_K
N_BLOCKS = N_SLOTS // EB + N_EXPERTS
N_ROWS = N_BLOCKS * EB
UNIT_BLOCKS = 6
UNIT_ROWS = UNIT_BLOCKS * EB
N_UNITS = N_EXPERTS + N_BLOCKS // UNIT_BLOCKS
TW = 256
NT1 = D_FF // TW
NT2 = D_MODEL // (2 * TW)
W_ROW_SPLIT = 8
WEIGHT_DMA_PRIORITY = 1
GATHER_UNROLL = 4
GATHER_PER_TILE = UNIT_ROWS // NT2
ROUTE_TILE = 512
COMB_TOK = 128

VMEM_LIMIT = 56 * 1024 * 1024


def _cparams(sem):
    return pltpu.CompilerParams(dimension_semantics=sem, vmem_limit_bytes=VMEM_LIMIT)


def _norm_kernel(xp_ref, xs_ref, g_ref, o_ref):
    i = pl.program_id(0)

    def body(x):
        y = x * lax.rsqrt(jnp.mean(x * x, axis=-1, keepdims=True) + EPS)
        o_ref[...] = (y * g_ref[...]).astype(o_ref.dtype)

    @pl.when(i < T_P // 512)
    def _():
        body(xp_ref[...])

    @pl.when(i >= T_P // 512)
    def _():
        body(xs_ref[...])


def _norm(xp, xs, g):
    np_ = T_P // 512
    return pl.pallas_call(
        _norm_kernel,
        grid=(T // 512,),
        in_specs=[
            pl.BlockSpec((512, D_MODEL), lambda i: (jnp.minimum(i, np_ - 1), 0)),
            pl.BlockSpec((512, D_MODEL), lambda i: (jnp.maximum(i - np_, 0), 0)),
            pl.BlockSpec((1, D_MODEL), lambda i: (0, 0)),
        ],
        out_specs=pl.BlockSpec((512, D_MODEL), lambda i: (i, 0)),
        out_shape=jax.ShapeDtypeStruct((T, D_MODEL), jnp.bfloat16),
        compiler_params=_cparams(("arbitrary",)),
        name="mix_norm",
    )(xp, xs, g)


def _conv3(ev, ck):
    return ck[2:3, :] * ev + ck[1:2, :] * pltpu.roll(ev, 1, 0) + ck[0:1, :] * pltpu.roll(ev, 2, 0)


def _window_sum(eu, group):
    s2 = eu + pltpu.roll(eu, 1, 0)
    s4 = s2 + pltpu.roll(s2, 2, 0)
    s8 = s4 + pltpu.roll(s4, 4, 0)
    s16 = s8 + pltpu.roll(s8, 8, 0)
    return jnp.where(group == 0, s2, jnp.where(group == 1, s4, jnp.where(group == 2, s8, s16)))


def _proj_kernel(xn_ref, wb_ref, wc_ref, wx_ref, wu_ref, ck_ref, cc_ref, cp_ref,
                 z_ref, pooled_ref, csp_ref, psp_ref, css_ref, pss_ref,
                 vcarry, ucarry, vs, us):
    c = pl.program_id(0)
    i = pl.program_id(1)
    xn = xn_ref[...]
    dot = functools.partial(jnp.dot, preferred_element_type=jnp.float32)
    gate_b = dot(xn, wb_ref[...])
    v = dot(xn, wc_ref[...]) * dot(xn, wx_ref[...])
    u = dot(xn, wu_ref[...])
    ck = ck_ref[...]
    group = c // (POOL_GROUP_DIM // TC)
    window = jnp.left_shift(2, group)

    @pl.when(i < N_PROMPT_TILES)
    def _prompt():
        first = (i % 2) == 0
        vh = jnp.where(first, 0.0, vcarry[...])
        uh = jnp.where(first, 0.0, ucarry[...])
        conv = _conv3(jnp.concatenate([vh, v], axis=0), ck)[CONV_PAD:]
        z_ref[...] = (gate_b * conv).astype(z_ref.dtype)
        win = _window_sum(jnp.concatenate([uh, u], axis=0), group)[POOL_PAD:]
        pos = (i % 2) * ROW_TILE + lax.broadcasted_iota(jnp.int32, (ROW_TILE, 1), 0)
        cnt = jnp.minimum(pos + 1, window).astype(jnp.float32)
        pooled_ref[...] = (win * (1.0 / cnt) - u).astype(pooled_ref.dtype)
        vcarry[...] = v[ROW_TILE - CONV_PAD:]
        ucarry[...] = u[ROW_TILE - POOL_PAD:]
        csp_ref[0] = vcarry[CONV_PAD - CONV_HIST:, :]
        psp_ref[0] = ucarry[POOL_PAD - POOL_HIST:, :]

    @pl.when(i >= N_PROMPT_TILES)
    def _sample():
        ev, eu = [], []
        for s in range(DEC_BATCH):
            ev += [cc_ref[s], v[s * DEC_SEQ:(s + 1) * DEC_SEQ]]
            eu += [cp_ref[s], u[s * DEC_SEQ:(s + 1) * DEC_SEQ]]
        conv_e = _conv3(jnp.concatenate(ev, axis=0), ck)
        win_e = _window_sum(jnp.concatenate(eu, axis=0), group)
        lv, lu = CONV_PAD + DEC_SEQ, POOL_PAD + DEC_SEQ
        conv = jnp.concatenate([conv_e[s * lv + CONV_PAD:(s + 1) * lv] for s in range(DEC_BATCH)], axis=0)
        win = jnp.concatenate([win_e[s * lu + POOL_PAD:(s + 1) * lu] for s in range(DEC_BATCH)], axis=0)
        z_ref[...] = (gate_b * conv).astype(z_ref.dtype)
        inv = 1.0 / window.astype(jnp.float32)
        pooled_ref[...] = (win * inv - u).astype(pooled_ref.dtype)
        vs[...] = v
        us[...] = u
        for s in range(DEC_BATCH):
            end = (s + 1) * DEC_SEQ
            css_ref[s] = vs[end - CONV_HIST:end, :]
            pss_ref[s] = us[end - POOL_HIST:end, :]


def _proj(xn, w_in_bf, conv_k, cache_conv_pad, cache_pool_pad):
    nsec = D_CONV // TC
    pidx = lambda c, i: (jnp.minimum(i, N_PROMPT_TILES - 1) // 2, 0, c)
    return pl.pallas_call(
        _proj_kernel,
        grid=(nsec, N_ROW_TILES),
        in_specs=[
            pl.BlockSpec((ROW_TILE, D_MODEL), lambda c, i: (i, 0)),
            pl.BlockSpec((D_MODEL, TC), lambda c, i: (0, c)),
            pl.BlockSpec((D_MODEL, TC), lambda c, i: (0, nsec + c)),
            pl.BlockSpec((D_MODEL, TC), lambda c, i: (0, 2 * nsec + c)),
            pl.BlockSpec((D_MODEL, TC), lambda c, i: (0, 3 * nsec + c)),
            pl.BlockSpec((3, TC), lambda c, i: (0, c)),
            pl.BlockSpec((DEC_BATCH, CONV_PAD, TC), lambda c, i: (0, 0, c)),
            pl.BlockSpec((DEC_BATCH, POOL_PAD, TC), lambda c, i: (0, 0, c)),
        ],
        out_specs=[
            pl.BlockSpec((ROW_TILE, TC), lambda c, i: (i, c)),
            pl.BlockSpec((ROW_TILE, TC), lambda c, i: (i, c)),
            pl.BlockSpec((1, CONV_HIST, TC), pidx),
            pl.BlockSpec((1, POOL_HIST, TC), pidx),
            pl.BlockSpec((DEC_BATCH, CONV_HIST, TC), lambda c, i: (0, 0, c)),
            pl.BlockSpec((DEC_BATCH, POOL_HIST, TC), lambda c, i: (0, 0, c)),
        ],
        out_shape=[
            jax.ShapeDtypeStruct((T, D_CONV), jnp.bfloat16),
            jax.ShapeDtypeStruct((T, D_POOL), jnp.bfloat16),
            jax.ShapeDtypeStruct((BATCH, CONV_HIST, D_CONV), jnp.float32),
            jax.ShapeDtypeStruct((BATCH, POOL_HIST, D_POOL), jnp.float32),
            jax.ShapeDtypeStruct((DEC_BATCH, CONV_HIST, D_CONV), jnp.float32),
            jax.ShapeDtypeStruct((DEC_BATCH, POOL_HIST, D_POOL), jnp.float32),
        ],
        scratch_shapes=[
            pltpu.VMEM((CONV_PAD, TC), jnp.float32),
            pltpu.VMEM((POOL_PAD, TC), jnp.float32),
            pltpu.VMEM((ROW_TILE, TC), jnp.float32),
            pltpu.VMEM((ROW_TILE, TC), jnp.float32),
        ],
        compiler_params=_cparams(("arbitrary", "arbitrary")),
        name="mix_proj",
    )(xn, w_in_bf, w_in_bf, w_in_bf, w_in_bf, conv_k, cache_conv_pad, cache_pool_pad)


def _gates_kernel(xn_ref, w_ref, o_ref):
    g = jnp.dot(xn_ref[...], w_ref[...], preferred_element_type=jnp.float32)
    o_ref[...] = jax.nn.sigmoid(g).astype(o_ref.dtype)


def _gates(xn, w_in_bf):
    tn = 1024
    off = (3 * D_CONV + D_POOL) // tn
    return pl.pallas_call(
        _gates_kernel,
        grid=(2 * D_MODEL // tn, N_ROW_TILES),
        in_specs=[
            pl.BlockSpec((ROW_TILE, D_MODEL), lambda j, i: (i, 0)),
            pl.BlockSpec((D_MODEL, tn), lambda j, i: (0, off + j)),
        ],
        out_specs=pl.BlockSpec((ROW_TILE, tn), lambda j, i: (i, j)),
        out_shape=jax.ShapeDtypeStruct((T, 2 * D_MODEL), jnp.bfloat16),
        compiler_params=_cparams(("arbitrary", "arbitrary")),
        name="mix_gates",
    )(xn, w_in_bf)


def _merge_kernel(z_ref, wco_ref, p_ref, wpm_ref, scale_ref, ga_ref, gb_ref, o_ref):
    y_conv = jnp.dot(z_ref[...], wco_ref[...], preferred_element_type=jnp.float32)
    y_pool = jnp.dot(p_ref[...], wpm_ref[0], preferred_element_type=jnp.float32) * scale_ref[...]
    mix = ga_ref[...].astype(jnp.float32) * y_conv + gb_ref[...].astype(jnp.float32) * y_pool
    o_ref[...] = mix.astype(o_ref.dtype)


def _merge(z, wco_bf, pooled, wpm_bf, pool_scale, sg):
    tn = D_MODEL // 4
    return pl.pallas_call(
        _merge_kernel,
        grid=(4, N_ROW_TILES),
        in_specs=[
            pl.BlockSpec((ROW_TILE, D_CONV), lambda j, i: (i, 0)),
            pl.BlockSpec((D_CONV, tn), lambda j, i: (0, j)),
            pl.BlockSpec((ROW_TILE, POOL_GROUP_DIM), lambda j, i: (i, j)),
            pl.BlockSpec((1, POOL_GROUP_DIM, tn), lambda j, i: (j, 0, 0)),
            pl.BlockSpec((1, tn), lambda j, i: (0, j)),
            pl.BlockSpec((ROW_TILE, tn), lambda j, i: (i, j)),
            pl.BlockSpec((ROW_TILE, tn), lambda j, i: (i, 4 + j)),
        ],
        out_specs=pl.BlockSpec((ROW_TILE, tn), lambda j, i: (i, j)),
        out_shape=jax.ShapeDtypeStruct((T, D_MODEL), jnp.bfloat16),
        compiler_params=_cparams(("arbitrary", "arbitrary")),
        name="mix_merge",
    )(z, wco_bf, pooled, wpm_bf, pool_scale, sg, sg)


def _oproj_kernel(m_ref, w_ref, xp_ref, xs_ref, o_ref):
    i = pl.program_id(1)
    a = jnp.dot(m_ref[...], w_ref[...], preferred_element_type=jnp.float32)

    @pl.when(i < T_P // 512)
    def _():
        o_ref[...] = xp_ref[...] + a

    @pl.when(i >= T_P // 512)
    def _():
        o_ref[...] = xs_ref[...] + a


def _oproj(mix, w_o_bf, xp, xs):
    tn = 1024
    np_ = T_P // 512
    return pl.pallas_call(
        _oproj_kernel,
        grid=(D_MODEL // tn, T // 512),
        in_specs=[
            pl.BlockSpec((512, D_MODEL), lambda j, i: (i, 0)),
            pl.BlockSpec((D_MODEL, tn), lambda j, i: (0, j)),
            pl.BlockSpec((512, tn), lambda j, i: (jnp.minimum(i, np_ - 1), j)),
            pl.BlockSpec((512, tn), lambda j, i: (jnp.maximum(i - np_, 0), j)),
        ],
        out_specs=pl.BlockSpec((512, tn), lambda j, i: (i, j)),
        out_shape=jax.ShapeDtypeStruct((T, D_MODEL), jnp.float32),
        compiler_params=_cparams(("arbitrary", "arbitrary")),
        name="mix_oproj",
    )(mix, w_o_bf, xp, xs)


def _pack_bf16_pair(lo, hi):
    lo_bits = pltpu.bitcast(lo.astype(jnp.bfloat16).astype(jnp.float32), jnp.uint32)
    hi_bits = pltpu.bitcast(hi.astype(jnp.bfloat16).astype(jnp.float32), jnp.uint32)
    return jnp.right_shift(lo_bits, jnp.uint32(16)) | (hi_bits & jnp.uint32(0xFFFF0000))


def _unpack_f32_pair(w):
    lo = pltpu.bitcast(jnp.left_shift(w, jnp.uint32(16)), jnp.float32)
    hi = pltpu.bitcast(w & jnp.uint32(0xFFFF0000), jnp.float32)
    return lo, hi


def _route_kernel(h_ref, g_ref, wr_ref, br_ref, hn_ref, idx_ref, gate_ref, rank_ref, hist_ref):
    h = h_ref[...]
    hn = h * lax.rsqrt(jnp.mean(h * h, axis=-1, keepdims=True) + EPS) * g_ref[...]
    hn_ref[...] = _pack_bf16_pair(hn[:, :HALF], hn[:, HALF:])
    logits = lax.dot_general(wr_ref[...], hn, (((1,), (1,)), ((), ())),
                             precision=lax.Precision.HIGHEST,
                             preferred_element_type=jnp.float32) + br_ref[...]
    eid = lax.broadcasted_iota(jnp.int32, logits.shape, 0)
    vals, ids = [], []
    for _ in range(TOP_K):
        m = jnp.max(logits, axis=0, keepdims=True)
        sel = jnp.min(jnp.where(logits == m, eid, N_EXPERTS), axis=0, keepdims=True)
        vals.append(m)
        ids.append(sel)
        logits = jnp.where(eid == sel, -jnp.inf, logits)
    ex = [jnp.exp(v - vals[0]) for v in vals]
    den = ex[0] + ex[1] + ex[2] + ex[3]
    for k in range(TOP_K):
        idx_ref[k:k + 1, :] = ids[k]
        gate_ref[k:k + 1, :] = ex[k] / den
    tm = logits.shape[1]
    tri = (lax.broadcasted_iota(jnp.int32, (tm, tm), 0)
           <= lax.broadcasted_iota(jnp.int32, (tm, tm), 1)).astype(jnp.bfloat16)
    run = jnp.zeros((N_EXPERTS, 1), jnp.float32)
    for k in range(TOP_K):
        oh = (eid == ids[k]).astype(jnp.float32)
        seen = jnp.dot(oh.astype(jnp.bfloat16), tri, preferred_element_type=jnp.float32)
        rank = jnp.sum(oh * (seen - 1.0 + run), axis=0, keepdims=True)
        rank_ref[k:k + 1, :] = rank.astype(jnp.int32)
        run = run + jnp.sum(oh, axis=1, keepdims=True)
    hist_ref[0] = jnp.broadcast_to(run, (N_EXPERTS, 128)).astype(jnp.int32)


def _route(h, g, wr_t, br):
    tm = ROUTE_TILE
    return pl.pallas_call(
        _route_kernel,
        grid=(T // tm,),
        in_specs=[
            pl.BlockSpec((tm, D_MODEL), lambda i: (i, 0)),
            pl.BlockSpec((1, D_MODEL), lambda i: (0, 0)),
            pl.BlockSpec((N_EXPERTS, D_MODEL), lambda i: (0, 0)),
            pl.BlockSpec((N_EXPERTS, 1), lambda i: (0, 0)),
        ],
        out_specs=[
            pl.BlockSpec((tm, HALF), lambda i: (i, 0)),
            pl.BlockSpec((TOP_K, tm), lambda i: (0, i)),
            pl.BlockSpec((TOP_K, tm), lambda i: (0, i)),
            pl.BlockSpec((TOP_K, tm), lambda i: (0, i)),
            pl.BlockSpec((1, N_EXPERTS, 128), lambda i: (i, 0, 0)),
        ],
        out_shape=[
            jax.ShapeDtypeStruct((T, HALF), jnp.uint32),
            jax.ShapeDtypeStruct((TOP_K, T), jnp.int32),
            jax.ShapeDtypeStruct((TOP_K, T), jnp.float32),
            jax.ShapeDtypeStruct((TOP_K, T), jnp.int32),
            jax.ShapeDtypeStruct((T // tm, N_EXPERTS, 128), jnp.int32),
        ],
        compiler_params=_cparams(("arbitrary",)),
        name="ffn_route",
    )(h, g, wr_t, br)


def _unpack_rows(xw):
    lo, hi = _unpack_f32_pair(xw)
    return lo.astype(jnp.bfloat16), hi.astype(jnp.bfloat16)


def _experts_kernel(ue_ref, ub_ref, un_ref, meta_ref, tok_ref,
                    bgu_ref, bd_ref, hn_hbm, wgu_hbm, wd_hbm, y_hbm,
                    xbuf, act, wstage, wbf, ystage, gsem, wsem, ysem):
    u = pl.program_id(0)
    n_units = meta_ref[0]
    n_used_blocks = meta_ref[1]
    dot = functools.partial(jnp.dot, preferred_element_type=jnp.float32)

    def gather_copy(unit, r):
        tok = tok_ref[ub_ref[unit] * EB + r]
        return pltpu.make_async_copy(hn_hbm.at[pl.ds(tok, 1), :], xbuf.at[pl.ds(r, 1), :], gsem)

    def start_gather(unit):
        def body(q, carry):
            for d in range(GATHER_UNROLL):
                gather_copy(unit, q * GATHER_UNROLL + d).start()
            return carry
        lax.fori_loop(0, UNIT_ROWS // GATHER_UNROLL, body, 0)

    def start_gather_part(unit, part):
        for d in range(GATHER_PER_TILE):
            gather_copy(unit, part * GATHER_PER_TILE + d).start()

    def wait_gather():
        pltpu.make_async_copy(hn_hbm.at[pl.ds(0, UNIT_ROWS), :], xbuf, gsem).wait()

    def tile_copies(w_hbm, e, cols, slot):
        out = []
        chunk = D_MODEL // W_ROW_SPLIT
        for part, c0 in enumerate(cols):
            for kh in range(W_ROW_SPLIT):
                rows = pl.ds(kh * chunk, chunk)
                out.append(pltpu.make_async_copy(w_hbm.at[e, rows, pl.ds(c0, TW)],
                                                 wstage.at[slot, part, rows, :], wsem.at[slot]))
        return out

    def gate_up_copies(e, j, slot):
        c0 = pl.multiple_of(j * TW, TW)
        return tile_copies(wgu_hbm, e, (c0, D_FF + c0), slot)

    def down_copies(e, j, slot):
        c0 = pl.multiple_of(j * 2 * TW, 2 * TW)
        return tile_copies(wd_hbm, e, (c0, c0 + TW), slot)

    def start(copies):
        for c in copies:
            c.start(priority=WEIGHT_DMA_PRIORITY)

    def wait_tile(slot):
        pltpu.make_async_copy(wgu_hbm.at[pl.ds(0, 2), :, pl.ds(0, TW)], wstage.at[slot], wsem.at[slot]).wait()

    def cast_tile(slot):
        wbf[:, :TW] = wstage[slot, 0].astype(jnp.bfloat16)
        wbf[:, TW:] = wstage[slot, 1].astype(jnp.bfloat16)

    def y_copy(block, j, ys):
        r0 = pl.multiple_of(block * EB, EB)
        c0 = pl.multiple_of(j * TW, TW)
        return pltpu.make_async_copy(ystage.at[ys], y_hbm.at[pl.ds(r0, UNIT_ROWS), pl.ds(c0, TW)],
                                     ysem.at[ys])

    def zero_copy(block, j):
        r0 = pl.multiple_of(block * EB, EB)
        c0 = pl.multiple_of(j * TW, TW)
        return pltpu.make_async_copy(ystage.at[0, pl.ds(0, EB), :],
                                     y_hbm.at[pl.ds(r0, EB), pl.ds(c0, TW)], ysem.at[0])

    def gate_up_rows(r0, j, bg, bl, m=EB):
        lo, hi = _unpack_rows(xbuf[pl.ds(r0, m), :])
        hcat = dot(jnp.concatenate([lo, hi], axis=1), wbf[...])
        glu = jnp.minimum(hcat[:, :TW] + bg, SWIGLU_LIMIT)
        lin = jnp.clip(hcat[:, TW:] + bl, -SWIGLU_LIMIT, SWIGLU_LIMIT)
        a = glu * jax.nn.sigmoid(SWIGLU_ALPHA * glu) * (lin + 1.0)
        act[j, pl.ds(r0, m), :] = a.astype(act.dtype)

    def down_rows(r0, ys, bd, m=EB):
        a = jnp.concatenate([act[jj, pl.ds(r0, m), :] for jj in range(NT1)], axis=1)
        y = dot(a, wbf[...]) + bd
        ystage[ys, pl.ds(r0, m), :] = _pack_bf16_pair(y[:, :TW], y[:, TW:])

    def for_blocks_after_first(nb, block_fn):
        rest = nb - 1
        for run in (4, 2, 1):
            @pl.when((rest & run) != 0)
            def _(run=run):
                first = 1 + (rest & ~(2 * run - 1))
                for q in range(run):
                    block_fn(pl.multiple_of((first + q) * EB, EB), EB)

    @pl.when(u == 0)
    def _prologue():
        ystage[...] = jnp.zeros(ystage.shape, ystage.dtype)
        start_gather(0)
        start(gate_up_copies(ue_ref[0], 0, 0))

    @pl.when(u < n_units)
    def _unit():
        e = ue_ref[u]
        b0 = ub_ref[u]
        nb = un_ref[u]
        has_next = u + 1 < n_units
        wait_gather()

        def gate_up_tile(j, carry):
            slot = j % 2
            wait_tile(slot)

            @pl.when(j + 1 < NT1)
            def _():
                start(gate_up_copies(e, j + 1, 1 - slot))

            @pl.when(j + 1 == NT1)
            def _():
                start(down_copies(e, 0, 1 - slot))

            bg = bgu_ref[0, pl.ds(j, 1), :]
            bl = bgu_ref[0, pl.ds(NT1 + j, 1), :]
            cast_tile(slot)
            gate_up_rows(0, j, bg, bl)
            for_blocks_after_first(nb, lambda r0, m: gate_up_rows(r0, j, bg, bl, m))
            return carry

        lax.fori_loop(0, NT1, gate_up_tile, 0)

        def down_tile(j, carry):
            slot = j % 2
            wait_tile(slot)

            @pl.when(j + 1 < NT2)
            def _():
                start(down_copies(e, j + 1, 1 - slot))

            @pl.when((j + 1 == NT2) & has_next)
            def _():
                start(gate_up_copies(ue_ref[u + 1], 0, 1 - slot))

            bd = bd_ref[0, pl.ds(j, 1), :]
            ys = j % 2

            @pl.when(j >= 2)
            def _():
                y_copy(b0, j, ys).wait()

            cast_tile(slot)
            start_gather_part(u + 1, j)
            down_rows(0, ys, bd)
            for_blocks_after_first(nb, lambda r0, m: down_rows(r0, ys, bd, m))
            y_copy(b0, j, ys).start()
            return carry

        lax.fori_loop(0, NT2, down_tile, 0)
        y_copy(b0, 0, 0).wait()
        y_copy(b0, 0, 1).wait()

        @pl.when(jnp.logical_not(has_next))
        def _():
            wait_gather()

    @pl.when(u == N_UNITS - 1)
    def _zero_unused_rows():
        ystage[0] = jnp.zeros(ystage.shape[1:], ystage.dtype)

        def body(b, carry):
            for j in range(NT2):
                zero_copy(b, j).start()
            for j in range(NT2):
                zero_copy(b, j).wait()
            return carry

        lax.fori_loop(n_used_blocks, N_BLOCKS + UNIT_BLOCKS, body, 0)


def _experts(unit_tables, row_tok, hn_packed, w_gate_up, b_gate_up, w_down, b_down):
    any_spec = pl.BlockSpec(memory_space=pl.ANY)
    return pl.pallas_call(
        _experts_kernel,
        grid_spec=pltpu.PrefetchScalarGridSpec(
            num_scalar_prefetch=5,
            grid=(N_UNITS,),
            in_specs=[
                pl.BlockSpec((1, 2 * NT1, TW), lambda u, ue, ub, un, meta, tok: (ue[u], 0, 0)),
                pl.BlockSpec((1, NT2, 2 * TW), lambda u, ue, ub, un, meta, tok: (ue[u], 0, 0)),
                any_spec, any_spec, any_spec,
            ],
            out_specs=any_spec,
            scratch_shapes=[
                pltpu.VMEM((UNIT_ROWS, HALF), jnp.uint32),
                pltpu.VMEM((NT1, UNIT_ROWS, TW), jnp.bfloat16),
                pltpu.VMEM((2, 2, D_MODEL, TW), jnp.float32),
                pltpu.VMEM((D_MODEL, 2 * TW), jnp.bfloat16),
                pltpu.VMEM((2, UNIT_ROWS, TW), jnp.uint32),
                pltpu.SemaphoreType.DMA(()),
                pltpu.SemaphoreType.DMA((2,)),
                pltpu.SemaphoreType.DMA((2,)),
            ],
        ),
        out_shape=jax.ShapeDtypeStruct((N_ROWS + UNIT_ROWS, HALF), jnp.uint32),
        compiler_params=_cparams(("arbitrary",)),
        name="ffn_experts",
    )(*unit_tables, row_tok, b_gate_up, b_down, hn_packed, w_gate_up, w_down)


def _combine_kernel(pos_ref, h_ref, gate_ref, g_ref, y_hbm, o_ref, rows, sem, *, tok0):
    i = pl.program_id(0)
    slot = i % 2

    def start_rows(step, dst_slot):
        base = tok0 + step * COMB_TOK

        def body(t, carry):
            for k in range(TOP_K):
                p = pos_ref[k * T + base + t]
                pltpu.make_async_copy(y_hbm.at[pl.ds(p, 1), :], rows.at[dst_slot, k, pl.ds(t, 1), :],
                                      sem.at[dst_slot]).start()
            return carry

        lax.fori_loop(0, COMB_TOK, body, 0)

    @pl.when(i == 0)
    def _():
        start_rows(0, 0)

    @pl.when(i + 1 < pl.num_programs(0))
    def _():
        start_rows(i + 1, 1 - slot)

    for k in range(TOP_K):
        pltpu.make_async_copy(y_hbm.at[pl.ds(0, COMB_TOK), :], rows.at[slot, k], sem.at[slot]).wait()
    gate = gate_ref[...]
    acc_lo = acc_hi = None
    for k in range(TOP_K):
        lo, hi = _unpack_f32_pair(rows[slot, k])
        g = gate[:, k:k + 1]
        acc_lo = lo * g if k == 0 else acc_lo + lo * g
        acc_hi = hi * g if k == 0 else acc_hi + hi * g
    pieces = []
    for j in range(NT2):
        pieces += [acc_lo[:, j * TW:(j + 1) * TW], acc_hi[:, j * TW:(j + 1) * TW]]
    x = h_ref[...] + jnp.concatenate(pieces, axis=1)
    y = x * lax.rsqrt(jnp.mean(x * x, axis=-1, keepdims=True) + EPS)
    o_ref[...] = y * g_ref[...]


def _combine(pos_flat, h, gate_tk, g_final, y_sorted, tok0, n_tok):
    blk0 = tok0 // COMB_TOK
    return pl.pallas_call(
        functools.partial(_combine_kernel, tok0=tok0),
        grid_spec=pltpu.PrefetchScalarGridSpec(
            num_scalar_prefetch=1,
            grid=(n_tok // COMB_TOK,),
            in_specs=[
                pl.BlockSpec((COMB_TOK, D_MODEL), lambda i, pos: (blk0 + i, 0)),
                pl.BlockSpec((COMB_TOK, TOP_K), lambda i, pos: (blk0 + i, 0)),
                pl.BlockSpec((1, D_MODEL), lambda i, pos: (0, 0)),
                pl.BlockSpec(memory_space=pl.ANY),
            ],
            out_specs=pl.BlockSpec((COMB_TOK, D_MODEL), lambda i, pos: (i, 0)),
            scratch_shapes=[
                pltpu.VMEM((2, TOP_K, COMB_TOK, HALF), jnp.uint32),
                pltpu.SemaphoreType.DMA((2,)),
            ],
        ),
        out_shape=jax.ShapeDtypeStruct((n_tok, D_MODEL), jnp.float32),
        compiler_params=_cparams(("arbitrary",)),
        name="ffn_combine",
    )(pos_flat, h, gate_tk, g_final, y_sorted)


def _routing(idx_t, rank_t, hist):
    i32 = jnp.int32
    tile_off = jnp.cumsum(hist, axis=0) - hist
    counts = jnp.sum(hist, axis=0)
    nblk = (counts + EB - 1) // EB
    blk_end = jnp.cumsum(nblk)
    blk_start = blk_end - nblk
    base_tab = tile_off + blk_start[None, :] * EB
    n_tiles = T // ROUTE_TILE
    onehot = idx_t.reshape(TOP_K, n_tiles, ROUTE_TILE, 1) == jnp.arange(N_EXPERTS, dtype=i32)
    base = jnp.sum(jnp.where(onehot, base_tab[None, :, None, :], 0), axis=-1)
    dest = (base.reshape(TOP_K, T) + rank_t).reshape(-1).astype(i32)
    tok = jnp.tile(jnp.arange(T, dtype=i32), TOP_K)
    row_tok = jnp.zeros((N_ROWS + UNIT_ROWS,), i32).at[dest].set(tok, unique_indices=True)

    n_unit_e = (nblk + UNIT_BLOCKS - 1) // UNIT_BLOCKS
    unit_end = jnp.cumsum(n_unit_e)
    n_units = unit_end[-1]
    u = jnp.minimum(jnp.arange(N_UNITS, dtype=i32), n_units - 1)
    e = jnp.minimum(jnp.sum(unit_end[None, :] <= u[:, None], axis=1), N_EXPERTS - 1).astype(i32)
    local = u - (unit_end[e] - n_unit_e[e])
    b0 = blk_start[e] + local * UNIT_BLOCKS
    nb = jnp.minimum(nblk[e] - local * UNIT_BLOCKS, UNIT_BLOCKS)
    meta = jnp.stack([n_units, blk_end[-1]]).astype(i32)
    return row_tok, dest, (e, b0.astype(i32), nb.astype(i32), meta)


def kernel(x_prompt, x_sample, cache_conv, cache_pool, norm_mix_g, w_in, conv_k, w_conv_out, w_pool_map,
           pool_scale, w_o, norm_ffn_g, w_router, b_router, w_gate_up, b_gate_up, w_down, b_down,
           norm_final_g):
    bf = jnp.bfloat16
    xp = x_prompt.reshape(T_P, D_MODEL)
    xs = x_sample.reshape(T_S, D_MODEL)
    cc_pad = jnp.pad(cache_conv[0], ((0, 0), (CONV_PAD - CONV_HIST, 0), (0, 0)))
    cp_pad = jnp.pad(cache_pool[0], ((0, 0), (POOL_PAD - POOL_HIST, 0), (0, 0)))

    xn = _norm(xp, xs, norm_mix_g)
    w_in_bf = w_in[0].astype(bf)
    z, pooled, cs_p, ps_p, cs_s, ps_s = _proj(xn, w_in_bf, conv_k[0], cc_pad, cp_pad)
    sg = _gates(xn, w_in_bf)
    mix = _merge(z, w_conv_out[0].astype(bf), pooled, w_pool_map[0].astype(bf), pool_scale, sg)
    h = _oproj(mix, w_o[0].astype(bf), xp, xs)

    hn_packed, idx_t, gate_t, rank_t, hist = _route(h, norm_ffn_g, w_router[0].T,
                                                    b_router[0].reshape(N_EXPERTS, 1))
    row_tok, dest, unit_tables = _routing(idx_t, rank_t, hist[:, :, 0])
    y_sorted = _experts(unit_tables, row_tok, hn_packed,
                        w_gate_up[0], b_gate_up[0].reshape(N_EXPERTS, 2 * NT1, TW),
                        w_down[0], b_down[0].reshape(N_EXPERTS, NT2, 2 * TW))

    gate_tk = gate_t.T
    g_final = norm_final_g.reshape(1, D_MODEL)
    y_p = _combine(dest, h, gate_tk, g_final, y_sorted, 0, T_P)
    y_s = _combine(dest, h, gate_tk, g_final, y_sorted, T_P, T_S)
    return (y_p.reshape(BATCH, SEQ, D_MODEL), y_s.reshape(DEC_BATCH, DEC_SEQ, D_MODEL),
            cs_p[None], ps_p[None], cs_s[None], ps_s[None])
```

```python
import functools

import jax
import jax.numpy as jnp
from jax import lax
from jax.experimental import pallas as pl
from jax.experimental.pallas import tpu as pltpu

D_MODEL = 4096
BATCH = 4
SEQ = 2048
DEC_BATCH = 16
DEC_SEQ = 64
D_CONV = D_MODEL // 2
D_POOL = D_MODEL // 2
CONV_HIST = 2
POOL_HIST = 15
POOL_GROUP_DIM = D_POOL // 4
N_EXPERTS = 32
TOP_K = 4
D_FF = D_MODEL
SWIGLU_ALPHA = 1.702
SWIGLU_LIMIT = 7.0
EPS = 1e-5

T_P = BATCH * SEQ
T_S = DEC_BATCH * DEC_SEQ
T = T_P + T_S
HALF = D_MODEL // 2

ROW_TILE = 1024
N_ROW_TILES = T // ROW_TILE
N_PROMPT_TILES = T_P // ROW_TILE
TC = 256
CONV_PAD = 8
POOL_PAD = 16

EB = 256
N_SLOTS = T * TOP_K
N_BLOCKS = N_SLOTS // EB + N_EXPERTS
N_ROWS = N_BLOCKS * EB
UNIT_BLOCKS = 6
UNIT_ROWS = UNIT_BLOCKS * EB
N_UNITS = N_EXPERTS + N_BLOCKS // UNIT_BLOCKS
TW = 256
NT1 = D_FF // TW
NT2 = D_MODEL // (2 * TW)
W_ROW_SPLIT = 8
GATHER_DMA_PRIORITY = 1
GATHER_UNROLL = 4
GATHER_PER_TILE = UNIT_ROWS // NT2
ROUTE_TILE = 512
COMB_TOK = 128

VMEM_LIMIT = 56 * 1024 * 1024


def _cparams(sem):
    return pltpu.CompilerParams(dimension_semantics=sem, vmem_limit_bytes=VMEM_LIMIT)


def _norm_kernel(xp_ref, xs_ref, g_ref, o_ref):
    i = pl.program_id(0)

    def body(x):
        y = x * lax.rsqrt(jnp.mean(x * x, axis=-1, keepdims=True) + EPS)
        o_ref[...] = (y * g_ref[...]).astype(o_ref.dtype)

    @pl.when(i < T_P // 512)
    def _():
        body(xp_ref[...])

    @pl.when(i >= T_P // 512)
    def _():
        body(xs_ref[...])


def _norm(xp, xs, g):
    np_ = T_P // 512
    return pl.pallas_call(
        _norm_kernel,
        grid=(T // 512,),
        in_specs=[
            pl.BlockSpec((512, D_MODEL), lambda i: (jnp.minimum(i, np_ - 1), 0)),
            pl.BlockSpec((512, D_MODEL), lambda i: (jnp.maximum(i - np_, 0), 0)),
            pl.BlockSpec((1, D_MODEL), lambda i: (0, 0)),
        ],
        out_specs=pl.BlockSpec((512, D_MODEL), lambda i: (i, 0)),
        out_shape=jax.ShapeDtypeStruct((T, D_MODEL), jnp.bfloat16),
        compiler_params=_cparams(("arbitrary",)),
        name="mix_norm",
    )(xp, xs, g)


def _conv3(ev, ck):
    return ck[2:3, :] * ev + ck[1:2, :] * pltpu.roll(ev, 1, 0) + ck[0:1, :] * pltpu.roll(ev, 2, 0)


def _window_sum(eu, group):
    s2 = eu + pltpu.roll(eu, 1, 0)
    s4 = s2 + pltpu.roll(s2, 2, 0)
    s8 = s4 + pltpu.roll(s4, 4, 0)
    s16 = s8 + pltpu.roll(s8, 8, 0)
    return jnp.where(group == 0, s2, jnp.where(group == 1, s4, jnp.where(group == 2, s8, s16)))


def _proj_kernel(xn_ref, wb_ref, wc_ref, wx_ref, wu_ref, ck_ref, cc_ref, cp_ref,
                 z_ref, pooled_ref, csp_ref, psp_ref, css_ref, pss_ref,
                 vcarry, ucarry, vs, us):
    c = pl.program_id(0)
    i = pl.program_id(1)
    xn = xn_ref[...]
    dot = functools.partial(jnp.dot, preferred_element_type=jnp.float32)
    gate_b = dot(xn, wb_ref[...])
    v = dot(xn, wc_ref[...]) * dot(xn, wx_ref[...])
    u = dot(xn, wu_ref[...])
    ck = ck_ref[...]
    group = c // (POOL_GROUP_DIM // TC)
    window = jnp.left_shift(2, group)

    @pl.when(i < N_PROMPT_TILES)
    def _prompt():
        first = (i % 2) == 0
        vh = jnp.where(first, 0.0, vcarry[...])
        uh = jnp.where(first, 0.0, ucarry[...])
        conv = _conv3(jnp.concatenate([vh, v], axis=0), ck)[CONV_PAD:]
        z_ref[...] = (gate_b * conv).astype(z_ref.dtype)
        win = _window_sum(jnp.concatenate([uh, u], axis=0), group)[POOL_PAD:]
        pos = (i % 2) * ROW_TILE + lax.broadcasted_iota(jnp.int32, (ROW_TILE, 1), 0)
        cnt = jnp.minimum(pos + 1, window).astype(jnp.float32)
        pooled_ref[...] = (win * (1.0 / cnt) - u).astype(pooled_ref.dtype)
        vcarry[...] = v[ROW_TILE - CONV_PAD:]
        ucarry[...] = u[ROW_TILE - POOL_PAD:]
        csp_ref[0] = vcarry[CONV_PAD - CONV_HIST:, :]
        psp_ref[0] = ucarry[POOL_PAD - POOL_HIST:, :]

    @pl.when(i >= N_PROMPT_TILES)
    def _sample():
        ev, eu = [], []
        for s in range(DEC_BATCH):
            ev += [cc_ref[s], v[s * DEC_SEQ:(s + 1) * DEC_SEQ]]
            eu += [cp_ref[s], u[s * DEC_SEQ:(s + 1) * DEC_SEQ]]
        conv_e = _conv3(jnp.concatenate(ev, axis=0), ck)
        win_e = _window_sum(jnp.concatenate(eu, axis=0), group)
        lv, lu = CONV_PAD + DEC_SEQ, POOL_PAD + DEC_SEQ
        conv = jnp.concatenate([conv_e[s * lv + CONV_PAD:(s + 1) * lv] for s in range(DEC_BATCH)], axis=0)
        win = jnp.concatenate([win_e[s * lu + POOL_PAD:(s + 1) * lu] for s in range(DEC_BATCH)], axis=0)
        z_ref[...] = (gate_b * conv).astype(z_ref.dtype)
        inv = 1.0 / window.astype(jnp.float32)
        pooled_ref[...] = (win * inv - u).astype(pooled_ref.dtype)
        vs[...] = v
        us[...] = u
        for s in range(DEC_BATCH):
            end = (s + 1) * DEC_SEQ
            css_ref[s] = vs[end - CONV_HIST:end, :]
            pss_ref[s] = us[end - POOL_HIST:end, :]


def _proj(xn, w_in_bf, conv_k, cache_conv_pad, cache_pool_pad):
    nsec = D_CONV // TC
    pidx = lambda c, i: (jnp.minimum(i, N_PROMPT_TILES - 1) // 2, 0, c)
    return pl.pallas_call(
        _proj_kernel,
        grid=(nsec, N_ROW_TILES),
        in_specs=[
            pl.BlockSpec((ROW_TILE, D_MODEL), lambda c, i: (i, 0)),
            pl.BlockSpec((D_MODEL, TC), lambda c, i: (0, c)),
            pl.BlockSpec((D_MODEL, TC), lambda c, i: (0, nsec + c)),
            pl.BlockSpec((D_MODEL, TC), lambda c, i: (0, 2 * nsec + c)),
            pl.BlockSpec((D_MODEL, TC), lambda c, i: (0, 3 * nsec + c)),
            pl.BlockSpec((3, TC), lambda c, i: (0, c)),
            pl.BlockSpec((DEC_BATCH, CONV_PAD, TC), lambda c, i: (0, 0, c)),
            pl.BlockSpec((DEC_BATCH, POOL_PAD, TC), lambda c, i: (0, 0, c)),
        ],
        out_specs=[
            pl.BlockSpec((ROW_TILE, TC), lambda c, i: (i, c)),
            pl.BlockSpec((ROW_TILE, TC), lambda c, i: (i, c)),
            pl.BlockSpec((1, CONV_HIST, TC), pidx),
            pl.BlockSpec((1, POOL_HIST, TC), pidx),
            pl.BlockSpec((DEC_BATCH, CONV_HIST, TC), lambda c, i: (0, 0, c)),
            pl.BlockSpec((DEC_BATCH, POOL_HIST, TC), lambda c, i: (0, 0, c)),
        ],
        out_shape=[
            jax.ShapeDtypeStruct((T, D_CONV), jnp.bfloat16),
            jax.ShapeDtypeStruct((T, D_POOL), jnp.bfloat16),
            jax.ShapeDtypeStruct((BATCH, CONV_HIST, D_CONV), jnp.float32),
            jax.ShapeDtypeStruct((BATCH, POOL_HIST, D_POOL), jnp.float32),
            jax.ShapeDtypeStruct((DEC_BATCH, CONV_HIST, D_CONV), jnp.float32),
            jax.ShapeDtypeStruct((DEC_BATCH, POOL_HIST, D_POOL), jnp.float32),
        ],
        scratch_shapes=[
            pltpu.VMEM((CONV_PAD, TC), jnp.float32),
            pltpu.VMEM((POOL_PAD, TC), jnp.float32),
            pltpu.VMEM((ROW_TILE, TC), jnp.float32),
            pltpu.VMEM((ROW_TILE, TC), jnp.float32),
        ],
        compiler_params=_cparams(("arbitrary", "arbitrary")),
        name="mix_proj",
    )(xn, w_in_bf, w_in_bf, w_in_bf, w_in_bf, conv_k, cache_conv_pad, cache_pool_pad)


def _gates_kernel(xn_ref, w_ref, o_ref):
    g = jnp.dot(xn_ref[...], w_ref[...], preferred_element_type=jnp.float32)
    o_ref[...] = jax.nn.sigmoid(g).astype(o_ref.dtype)


def _gates(xn, w_in_bf):
    tn = 1024
    off = (3 * D_CONV + D_POOL) // tn
    return pl.pallas_call(
        _gates_kernel,
        grid=(2 * D_MODEL // tn, N_ROW_TILES),
        in_specs=[
            pl.BlockSpec((ROW_TILE, D_MODEL), lambda j, i: (i, 0)),
            pl.BlockSpec((D_MODEL, tn), lambda j, i: (0, off + j)),
        ],
        out_specs=pl.BlockSpec((ROW_TILE, tn), lambda j, i: (i, j)),
        out_shape=jax.ShapeDtypeStruct((T, 2 * D_MODEL), jnp.bfloat16),
        compiler_params=_cparams(("arbitrary", "arbitrary")),
        name="mix_gates",
    )(xn, w_in_bf)


def _merge_kernel(z_ref, wco_ref, p_ref, wpm_ref, scale_ref, ga_ref, gb_ref, o_ref):
    y_conv = jnp.dot(z_ref[...], wco_ref[...], preferred_element_type=jnp.float32)
    y_pool = jnp.dot(p_ref[...], wpm_ref[0], preferred_element_type=jnp.float32) * scale_ref[...]
    mix = ga_ref[...].astype(jnp.float32) * y_conv + gb_ref[...].astype(jnp.float32) * y_pool
    o_ref[...] = mix.astype(o_ref.dtype)


def _merge(z, wco_bf, pooled, wpm_bf, pool_scale, sg):
    tn = D_MODEL // 4
    return pl.pallas_call(
        _merge_kernel,
        grid=(4, N_ROW_TILES),
        in_specs=[
            pl.BlockSpec((ROW_TILE, D_CONV), lambda j, i: (i, 0)),
            pl.BlockSpec((D_CONV, tn), lambda j, i: (0, j)),
            pl.BlockSpec((ROW_TILE, POOL_GROUP_DIM), lambda j, i: (i, j)),
            pl.BlockSpec((1, POOL_GROUP_DIM, tn), lambda j, i: (j, 0, 0)),
            pl.BlockSpec((1, tn), lambda j, i: (0, j)),
            pl.BlockSpec((ROW_TILE, tn), lambda j, i: (i, j)),
            pl.BlockSpec((ROW_TILE, tn), lambda j, i: (i, 4 + j)),
        ],
        out_specs=pl.BlockSpec((ROW_TILE, tn), lambda j, i: (i, j)),
        out_shape=jax.ShapeDtypeStruct((T, D_MODEL), jnp.bfloat16),
        compiler_params=_cparams(("arbitrary", "arbitrary")),
        name="mix_merge",
    )(z, wco_bf, pooled, wpm_bf, pool_scale, sg, sg)


def _oproj_kernel(m_ref, w_ref, xp_ref, xs_ref, o_ref):
    i = pl.program_id(1)
    a = jnp.dot(m_ref[...], w_ref[...], preferred_element_type=jnp.float32)

    @pl.when(i < T_P // 512)
    def _():
        o_ref[...] = xp_ref[...] + a

    @pl.when(i >= T_P // 512)
    def _():
        o_ref[...] = xs_ref[...] + a


def _oproj(mix, w_o_bf, xp, xs):
    tn = 1024
    np_ = T_P // 512
    return pl.pallas_call(
        _oproj_kernel,
        grid=(D_MODEL // tn, T // 512),
        in_specs=[
            pl.BlockSpec((512, D_MODEL), lambda j, i: (i, 0)),
            pl.BlockSpec((D_MODEL, tn), lambda j, i: (0, j)),
            pl.BlockSpec((512, tn), lambda j, i: (jnp.minimum(i, np_ - 1), j)),
            pl.BlockSpec((512, tn), lambda j, i: (jnp.maximum(i - np_, 0), j)),
        ],
        out_specs=pl.BlockSpec((512, tn), lambda j, i: (i, j)),
        out_shape=jax.ShapeDtypeStruct((T, D_MODEL), jnp.float32),
        compiler_params=_cparams(("arbitrary", "arbitrary")),
        name="mix_oproj",
    )(mix, w_o_bf, xp, xs)


def _pack_bf16_pair(lo, hi):
    lo_bits = pltpu.bitcast(lo.astype(jnp.bfloat16).astype(jnp.float32), jnp.uint32)
    hi_bits = pltpu.bitcast(hi.astype(jnp.bfloat16).astype(jnp.float32), jnp.uint32)
    return jnp.right_shift(lo_bits, jnp.uint32(16)) | (hi_bits & jnp.uint32(0xFFFF0000))


def _unpack_f32_pair(w):
    lo = pltpu.bitcast(jnp.left_shift(w, jnp.uint32(16)), jnp.float32)
    hi = pltpu.bitcast(w & jnp.uint32(0xFFFF0000), jnp.float32)
    return lo, hi


def _route_kernel(h_ref, g_ref, wr_ref, br_ref, hn_ref, idx_ref, gate_ref, rank_ref, hist_ref):
    h = h_ref[...]
    hn = h * lax.rsqrt(jnp.mean(h * h, axis=-1, keepdims=True) + EPS) * g_ref[...]
    hn_ref[...] = _pack_bf16_pair(hn[:, :HALF], hn[:, HALF:])
    logits = lax.dot_general(wr_ref[...], hn, (((1,), (1,)), ((), ())),
                             precision=lax.Precision.HIGHEST,
                             preferred_element_type=jnp.float32) + br_ref[...]
    eid = lax.broadcasted_iota(jnp.int32, logits.shape, 0)
    vals, ids = [], []
    for _ in range(TOP_K):
        m = jnp.max(logits, axis=0, keepdims=True)
        sel = jnp.min(jnp.where(logits == m, eid, N_EXPERTS), axis=0, keepdims=True)
        vals.append(m)
        ids.append(sel)
        logits = jnp.where(eid == sel, -jnp.inf, logits)
    ex = [jnp.exp(v - vals[0]) for v in vals]
    den = ex[0] + ex[1] + ex[2] + ex[3]
    for k in range(TOP_K):
        idx_ref[k:k + 1, :] = ids[k]
        gate_ref[k:k + 1, :] = ex[k] / den
    tm = logits.shape[1]
    tri = (lax.broadcasted_iota(jnp.int32, (tm, tm), 0)
           <= lax.broadcasted_iota(jnp.int32, (tm, tm), 1)).astype(jnp.bfloat16)
    run = jnp.zeros((N_EXPERTS, 1), jnp.float32)
    for k in range(TOP_K):
        oh = (eid == ids[k]).astype(jnp.float32)
        seen = jnp.dot(oh.astype(jnp.bfloat16), tri, preferred_element_type=jnp.float32)
        rank = jnp.sum(oh * (seen - 1.0 + run), axis=0, keepdims=True)
        rank_ref[k:k + 1, :] = rank.astype(jnp.int32)
        run = run + jnp.sum(oh, axis=1, keepdims=True)
    hist_ref[0] = jnp.broadcast_to(run, (N_EXPERTS, 128)).astype(jnp.int32)


def _route(h, g, wr_t, br):
    tm = ROUTE_TILE
    return pl.pallas_call(
        _route_kernel,
        grid=(T // tm,),
        in_specs=[
            pl.BlockSpec((tm, D_MODEL), lambda i: (i, 0)),
            pl.BlockSpec((1, D_MODEL), lambda i: (0, 0)),
            pl.BlockSpec((N_EXPERTS, D_MODEL), lambda i: (0, 0)),
            pl.BlockSpec((N_EXPERTS, 1), lambda i: (0, 0)),
        ],
        out_specs=[
            pl.BlockSpec((tm, HALF), lambda i: (i, 0)),
            pl.BlockSpec((TOP_K, tm), lambda i: (0, i)),
            pl.BlockSpec((TOP_K, tm), lambda i: (0, i)),
            pl.BlockSpec((TOP_K, tm), lambda i: (0, i)),
            pl.BlockSpec((1, N_EXPERTS, 128), lambda i: (i, 0, 0)),
        ],
        out_shape=[
            jax.ShapeDtypeStruct((T, HALF), jnp.uint32),
            jax.ShapeDtypeStruct((TOP_K, T), jnp.int32),
            jax.ShapeDtypeStruct((TOP_K, T), jnp.float32),
            jax.ShapeDtypeStruct((TOP_K, T), jnp.int32),
            jax.ShapeDtypeStruct((T // tm, N_EXPERTS, 128), jnp.int32),
        ],
        compiler_params=_cparams(("arbitrary",)),
        name="ffn_route",
    )(h, g, wr_t, br)


def _unpack_rows(xw):
    lo, hi = _unpack_f32_pair(xw)
    return lo.astype(jnp.bfloat16), hi.astype(jnp.bfloat16)


def _experts_kernel(ue_ref, ub_ref, un_ref, meta_ref, tok_ref,
                    bgu_ref, bd_ref, hn_hbm, wgu_hbm, wd_hbm, y_hbm,
                    xbuf, act, wstage, wbf, ystage, gsem, wsem, ysem):
    u = pl.program_id(0)
    n_units = meta_ref[0]
    n_used_blocks = meta_ref[1]
    dot = functools.partial(jnp.dot, preferred_element_type=jnp.float32)

    def gather_copy(unit, r):
        tok = tok_ref[ub_ref[unit] * EB + r]
        return pltpu.make_async_copy(hn_hbm.at[pl.ds(tok, 1), :], xbuf.at[pl.ds(r, 1), :], gsem)

    def start_gather(unit):
        def body(q, carry):
            for d in range(GATHER_UNROLL):
                gather_copy(unit, q * GATHER_UNROLL + d).start(priority=GATHER_DMA_PRIORITY)
            return carry
        lax.fori_loop(0, UNIT_ROWS // GATHER_UNROLL, body, 0)

    def start_gather_part(unit, part):
        for d in range(GATHER_PER_TILE):
            gather_copy(unit, part * GATHER_PER_TILE + d).start(priority=GATHER_DMA_PRIORITY)

    def wait_gather():
        pltpu.make_async_copy(hn_hbm.at[pl.ds(0, UNIT_ROWS), :], xbuf, gsem).wait()

    def tile_copies(w_hbm, e, cols, slot):
        out = []
        chunk = D_MODEL // W_ROW_SPLIT
        for part, c0 in enumerate(cols):
            for kh in range(W_ROW_SPLIT):
                rows = pl.ds(kh * chunk, chunk)
                out.append(pltpu.make_async_copy(w_hbm.at[e, rows, pl.ds(c0, TW)],
                                                 wstage.at[slot, part, rows, :], wsem.at[slot]))
        return out

    def gate_up_copies(e, j, slot):
        c0 = pl.multiple_of(j * TW, TW)
        return tile_copies(wgu_hbm, e, (c0, D_FF + c0), slot)

    def down_copies(e, j, slot):
        c0 = pl.multiple_of(j * 2 * TW, 2 * TW)
        return tile_copies(wd_hbm, e, (c0, c0 + TW), slot)

    def start(copies):
        for c in copies:
            c.start()

    def wait_tile(slot):
        pltpu.make_async_copy(wgu_hbm.at[pl.ds(0, 2), :, pl.ds(0, TW)], wstage.at[slot], wsem.at[slot]).wait()

    def cast_tile(slot):
        wbf[:, :TW] = wstage[slot, 0].astype(jnp.bfloat16)
        wbf[:, TW:] = wstage[slot, 1].astype(jnp.bfloat16)

    def y_copy(block, j, ys):
        r0 = pl.multiple_of(block * EB, EB)
        c0 = pl.multiple_of(j * TW, TW)
        return pltpu.make_async_copy(ystage.at[ys], y_hbm.at[pl.ds(r0, UNIT_ROWS), pl.ds(c0, TW)],
                                     ysem.at[ys])

    def zero_copy(block, j):
        r0 = pl.multiple_of(block * EB, EB)
        c0 = pl.multiple_of(j * TW, TW)
        return pltpu.make_async_copy(ystage.at[0, pl.ds(0, EB), :],
                                     y_hbm.at[pl.ds(r0, EB), pl.ds(c0, TW)], ysem.at[0])

    def gate_up_rows(r0, j, bg, bl, m=EB):
        lo, hi = _unpack_rows(xbuf[pl.ds(r0, m), :])
        hcat = dot(jnp.concatenate([lo, hi], axis=1), wbf[...])
        glu = jnp.minimum(hcat[:, :TW] + bg, SWIGLU_LIMIT)
        lin = jnp.clip(hcat[:, TW:] + bl, -SWIGLU_LIMIT, SWIGLU_LIMIT)
        a = glu * jax.nn.sigmoid(SWIGLU_ALPHA * glu) * (lin + 1.0)
        act[j, pl.ds(r0, m), :] = a.astype(act.dtype)

    def down_rows(r0, ys, bd, m=EB):
        a = jnp.concatenate([act[jj, pl.ds(r0, m), :] for jj in range(NT1)], axis=1)
        y = dot(a, wbf[...]) + bd
        ystage[ys, pl.ds(r0, m), :] = _pack_bf16_pair(y[:, :TW], y[:, TW:])

    def for_blocks_after_first(nb, block_fn):
        rest = nb - 1
        for run in (4, 2, 1):
            @pl.when((rest & run) != 0)
            def _(run=run):
                first = 1 + (rest & ~(2 * run - 1))
                for q in range(run):
                    block_fn(pl.multiple_of((first + q) * EB, EB), EB)

    @pl.when(u == 0)
    def _prologue():
        ystage[...] = jnp.zeros(ystage.shape, ystage.dtype)
        start_gather(0)
        start(gate_up_copies(ue_ref[0], 0, 0))

    @pl.when(u < n_units)
    def _unit():
        e = ue_ref[u]
        b0 = ub_ref[u]
        nb = un_ref[u]
        has_next = u + 1 < n_units
        wait_gather()

        def gate_up_tile(j, carry):
            slot = j % 2
            wait_tile(slot)

            @pl.when(j + 1 < NT1)
            def _():
                start(gate_up_copies(e, j + 1, 1 - slot))

            @pl.when(j + 1 == NT1)
            def _():
                start(down_copies(e, 0, 1 - slot))

            bg = bgu_ref[0, pl.ds(j, 1), :]
            bl = bgu_ref[0, pl.ds(NT1 + j, 1), :]
            cast_tile(slot)
            gate_up_rows(0, j, bg, bl)
            for_blocks_after_first(nb, lambda r0, m: gate_up_rows(r0, j, bg, bl, m))
            return carry

        lax.fori_loop(0, NT1, gate_up_tile, 0)

        def down_tile(j, carry):
            slot = j % 2
            wait_tile(slot)

            @pl.when(j + 1 < NT2)
            def _():
                start(down_copies(e, j + 1, 1 - slot))

            @pl.when((j + 1 == NT2) & has_next)
            def _():
                start(gate_up_copies(ue_ref[u + 1], 0, 1 - slot))

            bd = bd_ref[0, pl.ds(j, 1), :]
            ys = j % 2

            @pl.when(j >= 2)
            def _():
                y_copy(b0, j, ys).wait()

            cast_tile(slot)
            start_gather_part(u + 1, j)
            down_rows(0, ys, bd)
            for_blocks_after_first(nb, lambda r0, m: down_rows(r0, ys, bd, m))
            y_copy(b0, j, ys).start()
            return carry

        lax.fori_loop(0, NT2, down_tile, 0)
        y_copy(b0, 0, 0).wait()
        y_copy(b0, 0, 1).wait()

        @pl.when(jnp.logical_not(has_next))
        def _():
            wait_gather()

    @pl.when(u == N_UNITS - 1)
    def _zero_unused_rows():
        ystage[0] = jnp.zeros(ystage.shape[1:], ystage.dtype)

        def body(b, carry):
            for j in range(NT2):
                zero_copy(b, j).start()
            for j in range(NT2):
                zero_copy(b, j).wait()
            return carry

        lax.fori_loop(n_used_blocks, N_BLOCKS + UNIT_BLOCKS, body, 0)


def _experts(unit_tables, row_tok, hn_packed, w_gate_up, b_gate_up, w_down, b_down):
    any_spec = pl.BlockSpec(memory_space=pl.ANY)
    return pl.pallas_call(
        _experts_kernel,
        grid_spec=pltpu.PrefetchScalarGridSpec(
            num_scalar_prefetch=5,
            grid=(N_UNITS,),
            in_specs=[
                pl.BlockSpec((1, 2 * NT1, TW), lambda u, ue, ub, un, meta, tok: (ue[u], 0, 0)),
                pl.BlockSpec((1, NT2, 2 * TW), lambda u, ue, ub, un, meta, tok: (ue[u], 0, 0)),
                any_spec, any_spec, any_spec,
            ],
            out_specs=any_spec,
            scratch_shapes=[
                pltpu.VMEM((UNIT_ROWS, HALF), jnp.uint32),
                pltpu.VMEM((NT1, UNIT_ROWS, TW), jnp.bfloat16),
                pltpu.VMEM((2, 2, D_MODEL, TW), jnp.float32),
                pltpu.VMEM((D_MODEL, 2 * TW), jnp.bfloat16),
                pltpu.VMEM((2, UNIT_ROWS, TW), jnp.uint32),
                pltpu.SemaphoreType.DMA(()),
                pltpu.SemaphoreType.DMA((2,)),
                pltpu.SemaphoreType.DMA((2,)),
            ],
        ),
        out_shape=jax.ShapeDtypeStruct((N_ROWS + UNIT_ROWS, HALF), jnp.uint32),
        compiler_params=_cparams(("arbitrary",)),
        name="ffn_experts",
    )(*unit_tables, row_tok, b_gate_up, b_down, hn_packed, w_gate_up, w_down)


def _combine_kernel(pos_ref, h_ref, gate_ref, g_ref, y_hbm, o_ref, rows, sem, *, tok0):
    i = pl.program_id(0)
    slot = i % 2

    def start_rows(step, dst_slot):
        base = tok0 + step * COMB_TOK

        def body(t, carry):
            for k in range(TOP_K):
                p = pos_ref[k * T + base + t]
                pltpu.make_async_copy(y_hbm.at[pl.ds(p, 1), :], rows.at[dst_slot, k, pl.ds(t, 1), :],
                                      sem.at[dst_slot]).start()
            return carry

        lax.fori_loop(0, COMB_TOK, body, 0)

    @pl.when(i == 0)
    def _():
        start_rows(0, 0)

    @pl.when(i + 1 < pl.num_programs(0))
    def _():
        start_rows(i + 1, 1 - slot)

    for k in range(TOP_K):
        pltpu.make_async_copy(y_hbm.at[pl.ds(0, COMB_TOK), :], rows.at[slot, k], sem.at[slot]).wait()
    gate = gate_ref[...]
    acc_lo = acc_hi = None
    for k in range(TOP_K):
        lo, hi = _unpack_f32_pair(rows[slot, k])
        g = gate[:, k:k + 1]
        acc_lo = lo * g if k == 0 else acc_lo + lo * g
        acc_hi = hi * g if k == 0 else acc_hi + hi * g
    pieces = []
    for j in range(NT2):
        pieces += [acc_lo[:, j * TW:(j + 1) * TW], acc_hi[:, j * TW:(j + 1) * TW]]
    x = h_ref[...] + jnp.concatenate(pieces, axis=1)
    y = x * lax.rsqrt(jnp.mean(x * x, axis=-1, keepdims=True) + EPS)
    o_ref[...] = y * g_ref[...]


def _combine(pos_flat, h, gate_tk, g_final, y_sorted, tok0, n_tok):
    blk0 = tok0 // COMB_TOK
    return pl.pallas_call(
        functools.partial(_combine_kernel, tok0=tok0),
        grid_spec=pltpu.PrefetchScalarGridSpec(
            num_scalar_prefetch=1,
            grid=(n_tok // COMB_TOK,),
            in_specs=[
                pl.BlockSpec((COMB_TOK, D_MODEL), lambda i, pos: (blk0 + i, 0)),
                pl.BlockSpec((COMB_TOK, TOP_K), lambda i, pos: (blk0 + i, 0)),
                pl.BlockSpec((1, D_MODEL), lambda i, pos: (0, 0)),
                pl.BlockSpec(memory_space=pl.ANY),
            ],
            out_specs=pl.BlockSpec((COMB_TOK, D_MODEL), lambda i, pos: (i, 0)),
            scratch_shapes=[
                pltpu.VMEM((2, TOP_K, COMB_TOK, HALF), jnp.uint32),
                pltpu.SemaphoreType.DMA((2,)),
            ],
        ),
        out_shape=jax.ShapeDtypeStruct((n_tok, D_MODEL), jnp.float32),
        compiler_params=_cparams(("arbitrary",)),
        name="ffn_combine",
    )(pos_flat, h, gate_tk, g_final, y_sorted)


def _routing(idx_t, rank_t, hist):
    i32 = jnp.int32
    tile_off = jnp.cumsum(hist, axis=0) - hist
    counts = jnp.sum(hist, axis=0)
    nblk = (counts + EB - 1) // EB
    blk_end = jnp.cumsum(nblk)
    blk_start = blk_end - nblk
    base_tab = tile_off + blk_start[None, :] * EB
    n_tiles = T // ROUTE_TILE
    onehot = idx_t.reshape(TOP_K, n_tiles, ROUTE_TILE, 1) == jnp.arange(N_EXPERTS, dtype=i32)
    base = jnp.sum(jnp.where(onehot, base_tab[None, :, None, :], 0), axis=-1)
    dest = (base.reshape(TOP_K, T) + rank_t).reshape(-1).astype(i32)
    tok = jnp.tile(jnp.arange(T, dtype=i32), TOP_K)
    row_tok = jnp.zeros((N_ROWS + UNIT_ROWS,), i32).at[dest].set(tok, unique_indices=True)

    n_unit_e = (nblk + UNIT_BLOCKS - 1) // UNIT_BLOCKS
    unit_end = jnp.cumsum(n_unit_e)
    n_units = unit_end[-1]
    u = jnp.minimum(jnp.arange(N_UNITS, dtype=i32), n_units - 1)
    e = jnp.minimum(jnp.sum(unit_end[None, :] <= u[:, None], axis=1), N_EXPERTS - 1).astype(i32)
    local = u - (unit_end[e] - n_unit_e[e])
    b0 = blk_start[e] + local * UNIT_BLOCKS
    nb = jnp.minimum(nblk[e] - local * UNIT_BLOCKS, UNIT_BLOCKS)
    meta = jnp.stack([n_units, blk_end[-1]]).astype(i32)
    return row_tok, dest, (e, b0.astype(i32), nb.astype(i32), meta)


def kernel(x_prompt, x_sample, cache_conv, cache_pool, norm_mix_g, w_in, conv_k, w_conv_out, w_pool_map,
           pool_scale, w_o, norm_ffn_g, w_router, b_router, w_gate_up, b_gate_up, w_down, b_down,
           norm_final_g):
    bf = jnp.bfloat16
    xp = x_prompt.reshape(T_P, D_MODEL)
    xs = x_sample.reshape(T_S, D_MODEL)
    cc_pad = jnp.pad(cache_conv[0], ((0, 0), (CONV_PAD - CONV_HIST, 0), (0, 0)))
    cp_pad = jnp.pad(cache_pool[0], ((0, 0), (POOL_PAD - POOL_HIST, 0), (0, 0)))

    xn = _norm(xp, xs, norm_mix_g)
    w_in_bf = w_in[0].astype(bf)
    z, pooled, cs_p, ps_p, cs_s, ps_s = _proj(xn, w_in_bf, conv_k[0], cc_pad, cp_pad)
    sg = _gates(xn, w_in_bf)
    mix = _merge(z, w_conv_out[0].astype(bf), pooled, w_pool_map[0].astype(bf), pool_scale, sg)
    h = _oproj(mix, w_o[0].astype(bf), xp, xs)

    hn_packed, idx_t, gate_t, rank_t, hist = _route(h, norm_ffn_g, w_router[0].T,
                                                    b_router[0].reshape(N_EXPERTS, 1))
    row_tok, dest, unit_tables = _routing(idx_t, rank_t, hist[:, :, 0])
    y_sorted = _experts(unit_tables, row_tok, hn_packed,
                        w_gate_up[0], b_gate_up[0].reshape(N_EXPERTS, 2 * NT1, TW),
                        w_down[0], b_down[0].reshape(N_EXPERTS, NT2, 2 * TW))

    gate_tk = gate_t.T
    g_final = norm_final_g.reshape(1, D_MODEL)
    y_p = _combine(dest, h, gate_tk, g_final, y_sorted, 0, T_P)
    y_s = _combine(dest, h, gate_tk, g_final, y_sorted, T_P, T_S)
    return (y_p.reshape(BATCH, SEQ, D_MODEL), y_s.reshape(DEC_BATCH, DEC_SEQ, D_MODEL),
            cs_p[None], ps_p[None], cs_s[None], ps_s[None])
```

```python
import functools

import jax
import jax.numpy as jnp
from jax import lax
from jax.experimental import pallas as pl
from jax.experimental.pallas import tpu as pltpu

D_MODEL = 4096
BATCH = 4
SEQ = 2048
DEC_BATCH = 16
DEC_SEQ = 64
D_CONV = D_MODEL // 2
D_POOL = D_MODEL // 2
CONV_HIST = 2
POOL_HIST = 15
POOL_GROUP_DIM = D_POOL // 4
N_EXPERTS = 32
TOP_K = 4
D_FF = D_MODEL
SWIGLU_ALPHA = 1.702
SWIGLU_LIMIT = 7.0
EPS = 1e-5

T_P = BATCH * SEQ
T_S = DEC_BATCH * DEC_SEQ
T = T_P + T_S
HALF = D_MODEL // 2

ROW_TILE = 1024
N_ROW_TILES = T // ROW_TILE
N_PROMPT_TILES = T_P // ROW_TILE
TC = 256
CONV_PAD = 8
POOL_PAD = 16

EB = 256
N_SLOTS = T * TOP_K
N_BLOCKS = N_SLOTS // EB + N_EXPERTS
N_ROWS = N_BLOCKS * EB
UNIT_BLOCKS = 6
UNIT_ROWS = UNIT_BLOCKS * EB
N_UNITS = N_EXPERTS + N_BLOCKS // UNIT_BLOCKS
TW = 256
NT1 = D_FF // TW
NT2 = D_MODEL // (2 * TW)
W_ROW_SPLIT = 8
GATHER_UNROLL = 4
GATHER_ROWS = 5 * EB
GATHER_PER_TILE = GATHER_ROWS // NT2
ROUTE_TILE = 512
COMB_TOK = 128

VMEM_LIMIT = 56 * 1024 * 1024


def _cparams(sem):
    return pltpu.CompilerParams(dimension_semantics=sem, vmem_limit_bytes=VMEM_LIMIT)


def _norm_kernel(xp_ref, xs_ref, g_ref, o_ref):
    i = pl.program_id(0)

    def body(x):
        y = x * lax.rsqrt(jnp.mean(x * x, axis=-1, keepdims=True) + EPS)
        o_ref[...] = (y * g_ref[...]).astype(o_ref.dtype)

    @pl.when(i < T_P // 512)
    def _():
        body(xp_ref[...])

    @pl.when(i >= T_P // 512)
    def _():
        body(xs_ref[...])


def _norm(xp, xs, g):
    np_ = T_P // 512
    return pl.pallas_call(
        _norm_kernel,
        grid=(T // 512,),
        in_specs=[
            pl.BlockSpec((512, D_MODEL), lambda i: (jnp.minimum(i, np_ - 1), 0)),
            pl.BlockSpec((512, D_MODEL), lambda i: (jnp.maximum(i - np_, 0), 0)),
            pl.BlockSpec((1, D_MODEL), lambda i: (0, 0)),
        ],
        out_specs=pl.BlockSpec((512, D_MODEL), lambda i: (i, 0)),
        out_shape=jax.ShapeDtypeStruct((T, D_MODEL), jnp.bfloat16),
        compiler_params=_cparams(("arbitrary",)),
        name="mix_norm",
    )(xp, xs, g)


def _conv3(ev, ck):
    return ck[2:3, :] * ev + ck[1:2, :] * pltpu.roll(ev, 1, 0) + ck[0:1, :] * pltpu.roll(ev, 2, 0)


def _window_sum(eu, group):
    s2 = eu + pltpu.roll(eu, 1, 0)
    s4 = s2 + pltpu.roll(s2, 2, 0)
    s8 = s4 + pltpu.roll(s4, 4, 0)
    s16 = s8 + pltpu.roll(s8, 8, 0)
    return jnp.where(group == 0, s2, jnp.where(group == 1, s4, jnp.where(group == 2, s8, s16)))


def _proj_kernel(xn_ref, wb_ref, wc_ref, wx_ref, wu_ref, ck_ref, cc_ref, cp_ref,
                 z_ref, pooled_ref, csp_ref, psp_ref, css_ref, pss_ref,
                 vcarry, ucarry, vs, us):
    c = pl.program_id(0)
    i = pl.program_id(1)
    xn = xn_ref[...]
    dot = functools.partial(jnp.dot, preferred_element_type=jnp.float32)
    gate_b = dot(xn, wb_ref[...])
    v = dot(xn, wc_ref[...]) * dot(xn, wx_ref[...])
    u = dot(xn, wu_ref[...])
    ck = ck_ref[...]
    group = c // (POOL_GROUP_DIM // TC)
    window = jnp.left_shift(2, group)

    @pl.when(i < N_PROMPT_TILES)
    def _prompt():
        first = (i % 2) == 0
        vh = jnp.where(first, 0.0, vcarry[...])
        uh = jnp.where(first, 0.0, ucarry[...])
        conv = _conv3(jnp.concatenate([vh, v], axis=0), ck)[CONV_PAD:]
        z_ref[...] = (gate_b * conv).astype(z_ref.dtype)
        win = _window_sum(jnp.concatenate([uh, u], axis=0), group)[POOL_PAD:]
        pos = (i % 2) * ROW_TILE + lax.broadcasted_iota(jnp.int32, (ROW_TILE, 1), 0)
        cnt = jnp.minimum(pos + 1, window).astype(jnp.float32)
        pooled_ref[...] = (win * (1.0 / cnt) - u).astype(pooled_ref.dtype)
        vcarry[...] = v[ROW_TILE - CONV_PAD:]
        ucarry[...] = u[ROW_TILE - POOL_PAD:]
        csp_ref[0] = vcarry[CONV_PAD - CONV_HIST:, :]
        psp_ref[0] = ucarry[POOL_PAD - POOL_HIST:, :]

    @pl.when(i >= N_PROMPT_TILES)
    def _sample():
        ev, eu = [], []
        for s in range(DEC_BATCH):
            ev += [cc_ref[s], v[s * DEC_SEQ:(s + 1) * DEC_SEQ]]
            eu += [cp_ref[s], u[s * DEC_SEQ:(s + 1) * DEC_SEQ]]
        conv_e = _conv3(jnp.concatenate(ev, axis=0), ck)
        win_e = _window_sum(jnp.concatenate(eu, axis=0), group)
        lv, lu = CONV_PAD + DEC_SEQ, POOL_PAD + DEC_SEQ
        conv = jnp.concatenate([conv_e[s * lv + CONV_PAD:(s + 1) * lv] for s in range(DEC_BATCH)], axis=0)
        win = jnp.concatenate([win_e[s * lu + POOL_PAD:(s + 1) * lu] for s in range(DEC_BATCH)], axis=0)
        z_ref[...] = (gate_b * conv).astype(z_ref.dtype)
        inv = 1.0 / window.astype(jnp.float32)
        pooled_ref[...] = (win * inv - u).astype(pooled_ref.dtype)
        vs[...] = v
        us[...] = u
        for s in range(DEC_BATCH):
            end = (s + 1) * DEC_SEQ
            css_ref[s] = vs[end - CONV_HIST:end, :]
            pss_ref[s] = us[end - POOL_HIST:end, :]


def _proj(xn, w_in_bf, conv_k, cache_conv_pad, cache_pool_pad):
    nsec = D_CONV // TC
    pidx = lambda c, i: (jnp.minimum(i, N_PROMPT_TILES - 1) // 2, 0, c)
    return pl.pallas_call(
        _proj_kernel,
        grid=(nsec, N_ROW_TILES),
        in_specs=[
            pl.BlockSpec((ROW_TILE, D_MODEL), lambda c, i: (i, 0)),
            pl.BlockSpec((D_MODEL, TC), lambda c, i: (0, c)),
            pl.BlockSpec((D_MODEL, TC), lambda c, i: (0, nsec + c)),
            pl.BlockSpec((D_MODEL, TC), lambda c, i: (0, 2 * nsec + c)),
            pl.BlockSpec((D_MODEL, TC), lambda c, i: (0, 3 * nsec + c)),
            pl.BlockSpec((3, TC), lambda c, i: (0, c)),
            pl.BlockSpec((DEC_BATCH, CONV_PAD, TC), lambda c, i: (0, 0, c)),
            pl.BlockSpec((DEC_BATCH, POOL_PAD, TC), lambda c, i: (0, 0, c)),
        ],
        out_specs=[
            pl.BlockSpec((ROW_TILE, TC), lambda c, i: (i, c)),
            pl.BlockSpec((ROW_TILE, TC), lambda c, i: (i, c)),
            pl.BlockSpec((1, CONV_HIST, TC), pidx),
            pl.BlockSpec((1, POOL_HIST, TC), pidx),
            pl.BlockSpec((DEC_BATCH, CONV_HIST, TC), lambda c, i: (0, 0, c)),
            pl.BlockSpec((DEC_BATCH, POOL_HIST, TC), lambda c, i: (0, 0, c)),
        ],
        out_shape=[
            jax.ShapeDtypeStruct((T, D_CONV), jnp.bfloat16),
            jax.ShapeDtypeStruct((T, D_POOL), jnp.bfloat16),
            jax.ShapeDtypeStruct((BATCH, CONV_HIST, D_CONV), jnp.float32),
            jax.ShapeDtypeStruct((BATCH, POOL_HIST, D_POOL), jnp.float32),
            jax.ShapeDtypeStruct((DEC_BATCH, CONV_HIST, D_CONV), jnp.float32),
            jax.ShapeDtypeStruct((DEC_BATCH, POOL_HIST, D_POOL), jnp.float32),
        ],
        scratch_shapes=[
            pltpu.VMEM((CONV_PAD, TC), jnp.float32),
            pltpu.VMEM((POOL_PAD, TC), jnp.float32),
            pltpu.VMEM((ROW_TILE, TC), jnp.float32),
            pltpu.VMEM((ROW_TILE, TC), jnp.float32),
        ],
        compiler_params=_cparams(("arbitrary", "arbitrary")),
        name="mix_proj",
    )(xn, w_in_bf, w_in_bf, w_in_bf, w_in_bf, conv_k, cache_conv_pad, cache_pool_pad)


def _gates_kernel(xn_ref, w_ref, o_ref):
    g = jnp.dot(xn_ref[...], w_ref[...], preferred_element_type=jnp.float32)
    o_ref[...] = jax.nn.sigmoid(g).astype(o_ref.dtype)


def _gates(xn, w_in_bf):
    tn = 1024
    off = (3 * D_CONV + D_POOL) // tn
    return pl.pallas_call(
        _gates_kernel,
        grid=(2 * D_MODEL // tn, N_ROW_TILES),
        in_specs=[
            pl.BlockSpec((ROW_TILE, D_MODEL), lambda j, i: (i, 0)),
            pl.BlockSpec((D_MODEL, tn), lambda j, i: (0, off + j)),
        ],
        out_specs=pl.BlockSpec((ROW_TILE, tn), lambda j, i: (i, j)),
        out_shape=jax.ShapeDtypeStruct((T, 2 * D_MODEL), jnp.bfloat16),
        compiler_params=_cparams(("arbitrary", "arbitrary")),
        name="mix_gates",
    )(xn, w_in_bf)


def _merge_kernel(z_ref, wco_ref, p_ref, wpm_ref, scale_ref, ga_ref, gb_ref, o_ref):
    y_conv = jnp.dot(z_ref[...], wco_ref[...], preferred_element_type=jnp.float32)
    y_pool = jnp.dot(p_ref[...], wpm_ref[0], preferred_element_type=jnp.float32) * scale_ref[...]
    mix = ga_ref[...].astype(jnp.float32) * y_conv + gb_ref[...].astype(jnp.float32) * y_pool
    o_ref[...] = mix.astype(o_ref.dtype)


def _merge(z, wco_bf, pooled, wpm_bf, pool_scale, sg):
    tn = D_MODEL // 4
    return pl.pallas_call(
        _merge_kernel,
        grid=(4, N_ROW_TILES),
        in_specs=[
            pl.BlockSpec((ROW_TILE, D_CONV), lambda j, i: (i, 0)),
            pl.BlockSpec((D_CONV, tn), lambda j, i: (0, j)),
            pl.BlockSpec((ROW_TILE, POOL_GROUP_DIM), lambda j, i: (i, j)),
            pl.BlockSpec((1, POOL_GROUP_DIM, tn), lambda j, i: (j, 0, 0)),
            pl.BlockSpec((1, tn), lambda j, i: (0, j)),
            pl.BlockSpec((ROW_TILE, tn), lambda j, i: (i, j)),
            pl.BlockSpec((ROW_TILE, tn), lambda j, i: (i, 4 + j)),
        ],
        out_specs=pl.BlockSpec((ROW_TILE, tn), lambda j, i: (i, j)),
        out_shape=jax.ShapeDtypeStruct((T, D_MODEL), jnp.bfloat16),
        compiler_params=_cparams(("arbitrary", "arbitrary")),
        name="mix_merge",
    )(z, wco_bf, pooled, wpm_bf, pool_scale, sg, sg)


def _oproj_kernel(m_ref, w_ref, xp_ref, xs_ref, o_ref):
    i = pl.program_id(1)
    a = jnp.dot(m_ref[...], w_ref[...], preferred_element_type=jnp.float32)

    @pl.when(i < T_P // 512)
    def _():
        o_ref[...] = xp_ref[...] + a

    @pl.when(i >= T_P // 512)
    def _():
        o_ref[...] = xs_ref[...] + a


def _oproj(mix, w_o_bf, xp, xs):
    tn = 1024
    np_ = T_P // 512
    return pl.pallas_call(
        _oproj_kernel,
        grid=(D_MODEL // tn, T // 512),
        in_specs=[
            pl.BlockSpec((512, D_MODEL), lambda j, i: (i, 0)),
            pl.BlockSpec((D_MODEL, tn), lambda j, i: (0, j)),
            pl.BlockSpec((512, tn), lambda j, i: (jnp.minimum(i, np_ - 1), j)),
            pl.BlockSpec((512, tn), lambda j, i: (jnp.maximum(i - np_, 0), j)),
        ],
        out_specs=pl.BlockSpec((512, tn), lambda j, i: (i, j)),
        out_shape=jax.ShapeDtypeStruct((T, D_MODEL), jnp.float32),
        compiler_params=_cparams(("arbitrary", "arbitrary")),
        name="mix_oproj",
    )(mix, w_o_bf, xp, xs)


def _pack_bf16_pair(lo, hi):
    lo_bits = pltpu.bitcast(lo.astype(jnp.bfloat16).astype(jnp.float32), jnp.uint32)
    hi_bits = pltpu.bitcast(hi.astype(jnp.bfloat16).astype(jnp.float32), jnp.uint32)
    return jnp.right_shift(lo_bits, jnp.uint32(16)) | (hi_bits & jnp.uint32(0xFFFF0000))


def _unpack_f32_pair(w):
    lo = pltpu.bitcast(jnp.left_shift(w, jnp.uint32(16)), jnp.float32)
    hi = pltpu.bitcast(w & jnp.uint32(0xFFFF0000), jnp.float32)
    return lo, hi


def _route_kernel(h_ref, g_ref, wr_ref, br_ref, hn_ref, idx_ref, gate_ref, rank_ref, hist_ref):
    h = h_ref[...]
    hn = h * lax.rsqrt(jnp.mean(h * h, axis=-1, keepdims=True) + EPS) * g_ref[...]
    hn_ref[...] = _pack_bf16_pair(hn[:, :HALF], hn[:, HALF:])
    def split(x):
        hi = x.astype(jnp.bfloat16)
        return hi, (x - hi.astype(jnp.float32)).astype(jnp.bfloat16)

    def dot_nt(a, b):
        return lax.dot_general(a, b, (((1,), (1,)), ((), ())), preferred_element_type=jnp.float32)

    w_hi, w_lo = split(wr_ref[...])
    hn_hi, hn_lo = split(hn)
    logits = dot_nt(w_hi, hn_hi) + dot_nt(w_hi, hn_lo) + dot_nt(w_lo, hn_hi) + br_ref[...]
    eid = lax.broadcasted_iota(jnp.int32, logits.shape, 0)
    vals, ids = [], []
    for _ in range(TOP_K):
        m = jnp.max(logits, axis=0, keepdims=True)
        sel = jnp.min(jnp.where(logits == m, eid, N_EXPERTS), axis=0, keepdims=True)
        vals.append(m)
        ids.append(sel)
        logits = jnp.where(eid == sel, -jnp.inf, logits)
    ex = [jnp.exp(v - vals[0]) for v in vals]
    den = ex[0] + ex[1] + ex[2] + ex[3]
    for k in range(TOP_K):
        idx_ref[k:k + 1, :] = ids[k]
        gate_ref[k:k + 1, :] = ex[k] / den
    tm = logits.shape[1]
    tri = (lax.broadcasted_iota(jnp.int32, (tm, tm), 0)
           <= lax.broadcasted_iota(jnp.int32, (tm, tm), 1)).astype(jnp.bfloat16)
    run = jnp.zeros((N_EXPERTS, 1), jnp.float32)
    for k in range(TOP_K):
        oh = (eid == ids[k]).astype(jnp.float32)
        seen = jnp.dot(oh.astype(jnp.bfloat16), tri, preferred_element_type=jnp.float32)
        rank = jnp.sum(oh * (seen - 1.0 + run), axis=0, keepdims=True)
        rank_ref[k:k + 1, :] = rank.astype(jnp.int32)
        run = run + jnp.sum(oh, axis=1, keepdims=True)
    hist_ref[0] = jnp.broadcast_to(run, (N_EXPERTS, 128)).astype(jnp.int32)


def _route(h, g, wr_t, br):
    tm = ROUTE_TILE
    return pl.pallas_call(
        _route_kernel,
        grid=(T // tm,),
        in_specs=[
            pl.BlockSpec((tm, D_MODEL), lambda i: (i, 0)),
            pl.BlockSpec((1, D_MODEL), lambda i: (0, 0)),
            pl.BlockSpec((N_EXPERTS, D_MODEL), lambda i: (0, 0)),
            pl.BlockSpec((N_EXPERTS, 1), lambda i: (0, 0)),
        ],
        out_specs=[
            pl.BlockSpec((tm, HALF), lambda i: (i, 0)),
            pl.BlockSpec((TOP_K, tm), lambda i: (0, i)),
            pl.BlockSpec((TOP_K, tm), lambda i: (0, i)),
            pl.BlockSpec((TOP_K, tm), lambda i: (0, i)),
            pl.BlockSpec((1, N_EXPERTS, 128), lambda i: (i, 0, 0)),
        ],
        out_shape=[
            jax.ShapeDtypeStruct((T, HALF), jnp.uint32),
            jax.ShapeDtypeStruct((TOP_K, T), jnp.int32),
            jax.ShapeDtypeStruct((TOP_K, T), jnp.float32),
            jax.ShapeDtypeStruct((TOP_K, T), jnp.int32),
            jax.ShapeDtypeStruct((T // tm, N_EXPERTS, 128), jnp.int32),
        ],
        compiler_params=_cparams(("arbitrary",)),
        name="ffn_route",
    )(h, g, wr_t, br)


def _unpack_rows(xw):
    lo, hi = _unpack_f32_pair(xw)
    return lo.astype(jnp.bfloat16), hi.astype(jnp.bfloat16)


def _experts_kernel(ue_ref, ub_ref, un_ref, meta_ref, tok_ref,
                    bgu_ref, bd_ref, hn_hbm, wgu_hbm, wd_hbm, y_hbm,
                    xbuf, act, wstage, wbf, ystage, gsem, wsem, ysem):
    u = pl.program_id(0)
    n_units = meta_ref[0]
    n_used_blocks = meta_ref[1]
    dot = functools.partial(jnp.dot, preferred_element_type=jnp.float32)

    def gather_copy(unit, r):
        tok = tok_ref[ub_ref[unit] * EB + r]
        return pltpu.make_async_copy(hn_hbm.at[pl.ds(tok, 1), :], xbuf.at[pl.ds(r, 1), :], gsem)

    def start_gather_loop(unit, first, last):
        def body(q, carry):
            for d in range(GATHER_UNROLL):
                gather_copy(unit, q * GATHER_UNROLL + d).start()
            return carry
        lax.fori_loop(first // GATHER_UNROLL, last // GATHER_UNROLL, body, 0)

    def start_gather_rest(unit):
        @pl.when(un_ref[unit] * EB > GATHER_ROWS)
        def _():
            start_gather_loop(unit, GATHER_ROWS, UNIT_ROWS)

    def start_gather_part(unit, part):
        for d in range(GATHER_PER_TILE):
            gather_copy(unit, part * GATHER_PER_TILE + d).start()

    def wait_gather(unit):
        pltpu.make_async_copy(hn_hbm.at[pl.ds(0, GATHER_ROWS), :], xbuf.at[pl.ds(0, GATHER_ROWS), :], gsem).wait()

        @pl.when(un_ref[unit] * EB > GATHER_ROWS)
        def _():
            pltpu.make_async_copy(hn_hbm.at[pl.ds(0, UNIT_ROWS - GATHER_ROWS), :],
                                  xbuf.at[pl.ds(GATHER_ROWS, UNIT_ROWS - GATHER_ROWS), :], gsem).wait()

    def tile_copies(w_hbm, e, cols, slot):
        out = []
        chunk = D_MODEL // W_ROW_SPLIT
        for part, c0 in enumerate(cols):
            for kh in range(W_ROW_SPLIT):
                rows = pl.ds(kh * chunk, chunk)
                out.append(pltpu.make_async_copy(w_hbm.at[e, rows, pl.ds(c0, TW)],
                                                 wstage.at[slot, part, rows, :], wsem.at[slot]))
        return out

    def gate_up_copies(e, j, slot):
        c0 = pl.multiple_of(j * TW, TW)
        return tile_copies(wgu_hbm, e, (c0, D_FF + c0), slot)

    def down_copies(e, j, slot):
        c0 = pl.multiple_of(j * 2 * TW, 2 * TW)
        return tile_copies(wd_hbm, e, (c0, c0 + TW), slot)

    def start(copies):
        for c in copies:
            c.start()

    def wait_tile(slot):
        pltpu.make_async_copy(wgu_hbm.at[pl.ds(0, 2), :, pl.ds(0, TW)], wstage.at[slot], wsem.at[slot]).wait()

    def cast_tile(slot):
        wbf[:, :TW] = wstage[slot, 0].astype(jnp.bfloat16)
        wbf[:, TW:] = wstage[slot, 1].astype(jnp.bfloat16)

    def y_copy(block, j, ys):
        r0 = pl.multiple_of(block * EB, EB)
        c0 = pl.multiple_of(j * TW, TW)
        return pltpu.make_async_copy(ystage.at[ys], y_hbm.at[pl.ds(r0, UNIT_ROWS), pl.ds(c0, TW)],
                                     ysem.at[ys])

    def zero_copy(block, j):
        r0 = pl.multiple_of(block * EB, EB)
        c0 = pl.multiple_of(j * TW, TW)
        return pltpu.make_async_copy(ystage.at[0, pl.ds(0, EB), :],
                                     y_hbm.at[pl.ds(r0, EB), pl.ds(c0, TW)], ysem.at[0])

    def gate_up_rows(r0, j, bg, bl, m=EB):
        lo, hi = _unpack_rows(xbuf[pl.ds(r0, m), :])
        hcat = dot(jnp.concatenate([lo, hi], axis=1), wbf[...])
        glu = jnp.minimum(hcat[:, :TW] + bg, SWIGLU_LIMIT)
        lin = jnp.clip(hcat[:, TW:] + bl, -SWIGLU_LIMIT, SWIGLU_LIMIT)
        a = glu * jax.nn.sigmoid(SWIGLU_ALPHA * glu) * (lin + 1.0)
        act[j, pl.ds(r0, m), :] = a.astype(act.dtype)

    def down_rows(r0, ys, bd, m=EB):
        a = jnp.concatenate([act[jj, pl.ds(r0, m), :] for jj in range(NT1)], axis=1)
        y = dot(a, wbf[...]) + bd
        ystage[ys, pl.ds(r0, m), :] = _pack_bf16_pair(y[:, :TW], y[:, TW:])

    def for_blocks_after_first(nb, block_fn):
        rest = nb - 1
        for run in (4, 2, 1):
            @pl.when((rest & run) != 0)
            def _(run=run):
                first = 1 + (rest & ~(2 * run - 1))
                for q in range(run):
                    block_fn(pl.multiple_of((first + q) * EB, EB), EB)

    @pl.when(u == 0)
    def _prologue():
        ystage[...] = jnp.zeros(ystage.shape, ystage.dtype)
        start_gather_loop(0, 0, GATHER_ROWS)
        start_gather_rest(0)
        start(gate_up_copies(ue_ref[0], 0, 0))

    @pl.when(u < n_units)
    def _unit():
        e = ue_ref[u]
        b0 = ub_ref[u]
        nb = un_ref[u]
        has_next = u + 1 < n_units
        wait_gather(u)

        def gate_up_tile(j, carry):
            slot = j % 2
            wait_tile(slot)

            @pl.when(j + 1 < NT1)
            def _():
                start(gate_up_copies(e, j + 1, 1 - slot))

            @pl.when(j + 1 == NT1)
            def _():
                start(down_copies(e, 0, 1 - slot))

            bg = bgu_ref[0, pl.ds(j, 1), :]
            bl = bgu_ref[0, pl.ds(NT1 + j, 1), :]
            cast_tile(slot)
            gate_up_rows(0, j, bg, bl)
            for_blocks_after_first(nb, lambda r0, m: gate_up_rows(r0, j, bg, bl, m))
            return carry

        lax.fori_loop(0, NT1, gate_up_tile, 0)

        def down_tile(j, carry):
            slot = j % 2
            wait_tile(slot)

            @pl.when(j + 1 < NT2)
            def _():
                start(down_copies(e, j + 1, 1 - slot))

            @pl.when((j + 1 == NT2) & has_next)
            def _():
                start(gate_up_copies(ue_ref[u + 1], 0, 1 - slot))

            bd = bd_ref[0, pl.ds(j, 1), :]
            ys = j % 2

            @pl.when(j >= 2)
            def _():
                y_copy(b0, j, ys).wait()

            cast_tile(slot)
            start_gather_part(u + 1, j)
            down_rows(0, ys, bd)
            for_blocks_after_first(nb, lambda r0, m: down_rows(r0, ys, bd, m))
            y_copy(b0, j, ys).start()
            return carry

        lax.fori_loop(0, NT2, down_tile, 0)
        y_copy(b0, 0, 0).wait()
        y_copy(b0, 0, 1).wait()
        start_gather_rest(u + 1)

        @pl.when(jnp.logical_not(has_next))
        def _():
            wait_gather(u + 1)

    @pl.when(u == N_UNITS - 1)
    def _zero_unused_rows():
        ystage[0] = jnp.zeros(ystage.shape[1:], ystage.dtype)

        def body(b, carry):
            for j in range(NT2):
                zero_copy(b, j).start()
            for j in range(NT2):
                zero_copy(b, j).wait()
            return carry

        lax.fori_loop(n_used_blocks, N_BLOCKS + UNIT_BLOCKS, body, 0)


def _experts(unit_tables, row_tok, hn_packed, w_gate_up, b_gate_up, w_down, b_down):
    any_spec = pl.BlockSpec(memory_space=pl.ANY)
    return pl.pallas_call(
        _experts_kernel,
        grid_spec=pltpu.PrefetchScalarGridSpec(
            num_scalar_prefetch=5,
            grid=(N_UNITS,),
            in_specs=[
                pl.BlockSpec((1, 2 * NT1, TW), lambda u, ue, ub, un, meta, tok: (ue[u], 0, 0)),
                pl.BlockSpec((1, NT2, 2 * TW), lambda u, ue, ub, un, meta, tok: (ue[u], 0, 0)),
                any_spec, any_spec, any_spec,
            ],
            out_specs=any_spec,
            scratch_shapes=[
                pltpu.VMEM((UNIT_ROWS, HALF), jnp.uint32),
                pltpu.VMEM((NT1, UNIT_ROWS, TW), jnp.bfloat16),
                pltpu.VMEM((2, 2, D_MODEL, TW), jnp.float32),
                pltpu.VMEM((D_MODEL, 2 * TW), jnp.bfloat16),
                pltpu.VMEM((2, UNIT_ROWS, TW), jnp.uint32),
                pltpu.SemaphoreType.DMA(()),
                pltpu.SemaphoreType.DMA((2,)),
                pltpu.SemaphoreType.DMA((2,)),
            ],
        ),
        out_shape=jax.ShapeDtypeStruct((N_ROWS + UNIT_ROWS, HALF), jnp.uint32),
        compiler_params=_cparams(("arbitrary",)),
        name="ffn_experts",
    )(*unit_tables, row_tok, b_gate_up, b_down, hn_packed, w_gate_up, w_down)


def _combine_kernel(pos_ref, h_ref, gate_ref, g_ref, y_hbm, o_ref, rows, sem, *, tok0):
    i = pl.program_id(0)
    slot = i % 2

    def start_rows(step, dst_slot):
        base = tok0 + step * COMB_TOK

        def body(t, carry):
            for k in range(TOP_K):
                p = pos_ref[k * T + base + t]
                pltpu.make_async_copy(y_hbm.at[pl.ds(p, 1), :], rows.at[dst_slot, k, pl.ds(t, 1), :],
                                      sem.at[dst_slot]).start()
            return carry

        lax.fori_loop(0, COMB_TOK, body, 0)

    @pl.when(i == 0)
    def _():
        start_rows(0, 0)

    @pl.when(i + 1 < pl.num_programs(0))
    def _():
        start_rows(i + 1, 1 - slot)

    for k in range(TOP_K):
        pltpu.make_async_copy(y_hbm.at[pl.ds(0, COMB_TOK), :], rows.at[slot, k], sem.at[slot]).wait()
    gate = gate_ref[...]
    acc_lo = acc_hi = None
    for k in range(TOP_K):
        lo, hi = _unpack_f32_pair(rows[slot, k])
        g = gate[:, k:k + 1]
        acc_lo = lo * g if k == 0 else acc_lo + lo * g
        acc_hi = hi * g if k == 0 else acc_hi + hi * g
    pieces = []
    for j in range(NT2):
        pieces += [acc_lo[:, j * TW:(j + 1) * TW], acc_hi[:, j * TW:(j + 1) * TW]]
    x = h_ref[...] + jnp.concatenate(pieces, axis=1)
    y = x * lax.rsqrt(jnp.mean(x * x, axis=-1, keepdims=True) + EPS)
    o_ref[...] = y * g_ref[...]


def _combine(pos_flat, h, gate_tk, g_final, y_sorted, tok0, n_tok):
    blk0 = tok0 // COMB_TOK
    return pl.pallas_call(
        functools.partial(_combine_kernel, tok0=tok0),
        grid_spec=pltpu.PrefetchScalarGridSpec(
            num_scalar_prefetch=1,
            grid=(n_tok // COMB_TOK,),
            in_specs=[
                pl.BlockSpec((COMB_TOK, D_MODEL), lambda i, pos: (blk0 + i, 0)),
                pl.BlockSpec((COMB_TOK, TOP_K), lambda i, pos: (blk0 + i, 0)),
                pl.BlockSpec((1, D_MODEL), lambda i, pos: (0, 0)),
                pl.BlockSpec(memory_space=pl.ANY),
            ],
            out_specs=pl.BlockSpec((COMB_TOK, D_MODEL), lambda i, pos: (i, 0)),
            scratch_shapes=[
                pltpu.VMEM((2, TOP_K, COMB_TOK, HALF), jnp.uint32),
                pltpu.SemaphoreType.DMA((2,)),
            ],
        ),
        out_shape=jax.ShapeDtypeStruct((n_tok, D_MODEL), jnp.float32),
        compiler_params=_cparams(("arbitrary",)),
        name="ffn_combine",
    )(pos_flat, h, gate_tk, g_final, y_sorted)


def _routing(idx_t, rank_t, hist):
    i32 = jnp.int32
    tile_off = jnp.cumsum(hist, axis=0) - hist
    counts = jnp.sum(hist, axis=0)
    nblk = (counts + EB - 1) // EB
    blk_end = jnp.cumsum(nblk)
    blk_start = blk_end - nblk
    base_tab = tile_off + blk_start[None, :] * EB
    n_tiles = T // ROUTE_TILE
    onehot = idx_t.reshape(TOP_K, n_tiles, ROUTE_TILE, 1) == jnp.arange(N_EXPERTS, dtype=i32)
    base = jnp.sum(jnp.where(onehot, base_tab[None, :, None, :], 0), axis=-1)
    dest = (base.reshape(TOP_K, T) + rank_t).reshape(-1).astype(i32)
    tok = jnp.tile(jnp.arange(T, dtype=i32), TOP_K)
    row_tok = jnp.zeros((N_ROWS + UNIT_ROWS,), i32).at[dest].set(tok, unique_indices=True)

    n_unit_e = (nblk + UNIT_BLOCKS - 1) // UNIT_BLOCKS
    unit_end = jnp.cumsum(n_unit_e)
    n_units = unit_end[-1]
    u = jnp.minimum(jnp.arange(N_UNITS, dtype=i32), n_units - 1)
    e = jnp.minimum(jnp.sum(unit_end[None, :] <= u[:, None], axis=1), N_EXPERTS - 1).astype(i32)
    local = u - (unit_end[e] - n_unit_e[e])
    b0 = blk_start[e] + local * UNIT_BLOCKS
    nb = jnp.minimum(nblk[e] - local * UNIT_BLOCKS, UNIT_BLOCKS)
    meta = jnp.stack([n_units, blk_end[-1]]).astype(i32)
    return row_tok, dest, (e, b0.astype(i32), nb.astype(i32), meta)


def kernel(x_prompt, x_sample, cache_conv, cache_pool, norm_mix_g, w_in, conv_k, w_conv_out, w_pool_map,
           pool_scale, w_o, norm_ffn_g, w_router, b_router, w_gate_up, b_gate_up, w_down, b_down,
           norm_final_g):
    bf = jnp.bfloat16
    xp = x_prompt.reshape(T_P, D_MODEL)
    xs = x_sample.reshape(T_S, D_MODEL)
    cc_pad = jnp.pad(cache_conv[0], ((0, 0), (CONV_PAD - CONV_HIST, 0), (0, 0)))
    cp_pad = jnp.pad(cache_pool[0], ((0, 0), (POOL_PAD - POOL_HIST, 0), (0, 0)))

    xn = _norm(xp, xs, norm_mix_g)
    w_in_bf = w_in[0].astype(bf)
    z, pooled, cs_p, ps_p, cs_s, ps_s = _proj(xn, w_in_bf, conv_k[0], cc_pad, cp_pad)
    sg = _gates(xn, w_in_bf)
    mix = _merge(z, w_conv_out[0].astype(bf), pooled, w_pool_map[0].astype(bf), pool_scale, sg)
    h = _oproj(mix, w_o[0].astype(bf), xp, xs)

    hn_packed, idx_t, gate_t, rank_t, hist = _route(h, norm_ffn_g, w_router[0].T,
                                                    b_router[0].reshape(N_EXPERTS, 1))
    row_tok, dest, unit_tables = _routing(idx_t, rank_t, hist[:, :, 0])
    y_sorted = _experts(unit_tables, row_tok, hn_packed,
                        w_gate_up[0], b_gate_up[0].reshape(N_EXPERTS, 2 * NT1, TW),
                        w_down[0], b_down[0].reshape(N_EXPERTS, NT2, 2 * TW))

    gate_tk = gate_t.T
    g_final = norm_final_g.reshape(1, D_MODEL)
    y_p = _combine(dest, h, gate_tk, g_final, y_sorted, 0, T_P)
    y_s = _combine(dest, h, gate_tk, g_final, y_sorted, T_P, T_S)
    return (y_p.reshape(BATCH, SEQ, D_MODEL), y_s.reshape(DEC_BATCH, DEC_SEQ, D_MODEL),
            cs_p[None], ps_p[None], cs_s[None], ps_s[None])
```

```python
import functools

import jax
import jax.numpy as jnp
from jax import lax
from jax.experimental import pallas as pl
from jax.experimental.pallas import tpu as pltpu

D_MODEL = 4096
BATCH = 4
SEQ = 2048
DEC_BATCH = 16
DEC_SEQ = 64
D_CONV = D_MODEL // 2
D_POOL = D_MODEL // 2
CONV_HIST = 2
POOL_HIST = 15
POOL_GROUP_DIM = D_POOL // 4
N_EXPERTS = 32
TOP_K = 4
D_FF = D_MODEL
SWIGLU_ALPHA = 1.702
SWIGLU_LIMIT = 7.0
EPS = 1e-5

T_P = BATCH * SEQ
T_S = DEC_BATCH * DEC_SEQ
T = T_P + T_S
HALF = D_MODEL // 2

ROW_TILE = 1024
N_ROW_TILES = T // ROW_TILE
N_PROMPT_TILES = T_P // ROW_TILE
TC = 256
CONV_PAD = 8
POOL_PAD = 16

EB = 256
N_SLOTS = T * TOP_K
N_BLOCKS = N_SLOTS // EB + N_EXPERTS
N_ROWS = N_BLOCKS * EB
UNIT_BLOCKS = 6
UNIT_ROWS = UNIT_BLOCKS * EB
N_UNITS = N_EXPERTS + N_BLOCKS // UNIT_BLOCKS
TW = 256
NT1 = D_FF // TW
NT2 = D_MODEL // (2 * TW)
W_ROW_SPLIT = 8
GATHER_UNROLL = 4
GATHER_PER_TILE = UNIT_ROWS // NT2
ROUTE_TILE = 512
COMB_TOK = 128

VMEM_LIMIT = 56 * 1024 * 1024


def _cparams(sem):
    return pltpu.CompilerParams(dimension_semantics=sem, vmem_limit_bytes=VMEM_LIMIT)


def _norm_kernel(xp_ref, xs_ref, g_ref, o_ref):
    i = pl.program_id(0)

    def body(x):
        y = x * lax.rsqrt(jnp.mean(x * x, axis=-1, keepdims=True) + EPS)
        o_ref[...] = (y * g_ref[...]).astype(o_ref.dtype)

    @pl.when(i < T_P // 512)
    def _():
        body(xp_ref[...])

    @pl.when(i >= T_P // 512)
    def _():
        body(xs_ref[...])


def _norm(xp, xs, g):
    np_ = T_P // 512
    return pl.pallas_call(
        _norm_kernel,
        grid=(T // 512,),
        in_specs=[
            pl.BlockSpec((512, D_MODEL), lambda i: (jnp.minimum(i, np_ - 1), 0)),
            pl.BlockSpec((512, D_MODEL), lambda i: (jnp.maximum(i - np_, 0), 0)),
            pl.BlockSpec((1, D_MODEL), lambda i: (0, 0)),
        ],
        out_specs=pl.BlockSpec((512, D_MODEL), lambda i: (i, 0)),
        out_shape=jax.ShapeDtypeStruct((T, D_MODEL), jnp.bfloat16),
        compiler_params=_cparams(("arbitrary",)),
        name="mix_norm",
    )(xp, xs, g)


def _conv3(ev, ck):
    return ck[2:3, :] * ev + ck[1:2, :] * pltpu.roll(ev, 1, 0) + ck[0:1, :] * pltpu.roll(ev, 2, 0)


def _window_sum(eu, group):
    s2 = eu + pltpu.roll(eu, 1, 0)
    s4 = s2 + pltpu.roll(s2, 2, 0)
    s8 = s4 + pltpu.roll(s4, 4, 0)
    s16 = s8 + pltpu.roll(s8, 8, 0)
    return jnp.where(group == 0, s2, jnp.where(group == 1, s4, jnp.where(group == 2, s8, s16)))


def _proj_kernel(xn_ref, wb_ref, wc_ref, wx_ref, wu_ref, ck_ref, cc_ref, cp_ref,
                 z_ref, pooled_ref, csp_ref, psp_ref, css_ref, pss_ref,
                 vcarry, ucarry, vs, us):
    c = pl.program_id(0)
    i = pl.program_id(1)
    xn = xn_ref[...]
    dot = functools.partial(jnp.dot, preferred_element_type=jnp.float32)
    gate_b = dot(xn, wb_ref[...])
    v = dot(xn, wc_ref[...]) * dot(xn, wx_ref[...])
    u = dot(xn, wu_ref[...])
    ck = ck_ref[...]
    group = c // (POOL_GROUP_DIM // TC)
    window = jnp.left_shift(2, group)

    @pl.when(i < N_PROMPT_TILES)
    def _prompt():
        first = (i % 2) == 0
        vh = jnp.where(first, 0.0, vcarry[...])
        uh = jnp.where(first, 0.0, ucarry[...])
        conv = _conv3(jnp.concatenate([vh, v], axis=0), ck)[CONV_PAD:]
        z_ref[...] = (gate_b * conv).astype(z_ref.dtype)
        win = _window_sum(jnp.concatenate([uh, u], axis=0), group)[POOL_PAD:]
        pos = (i % 2) * ROW_TILE + lax.broadcasted_iota(jnp.int32, (ROW_TILE, 1), 0)
        cnt = jnp.minimum(pos + 1, window).astype(jnp.float32)
        pooled_ref[...] = (win * (1.0 / cnt) - u).astype(pooled_ref.dtype)
        vcarry[...] = v[ROW_TILE - CONV_PAD:]
        ucarry[...] = u[ROW_TILE - POOL_PAD:]
        csp_ref[0] = vcarry[CONV_PAD - CONV_HIST:, :]
        psp_ref[0] = ucarry[POOL_PAD - POOL_HIST:, :]

    @pl.when(i >= N_PROMPT_TILES)
    def _sample():
        ev, eu = [], []
        for s in range(DEC_BATCH):
            ev += [cc_ref[s], v[s * DEC_SEQ:(s + 1) * DEC_SEQ]]
            eu += [cp_ref[s], u[s * DEC_SEQ:(s + 1) * DEC_SEQ]]
        conv_e = _conv3(jnp.concatenate(ev, axis=0), ck)
        win_e = _window_sum(jnp.concatenate(eu, axis=0), group)
        lv, lu = CONV_PAD + DEC_SEQ, POOL_PAD + DEC_SEQ
        conv = jnp.concatenate([conv_e[s * lv + CONV_PAD:(s + 1) * lv] for s in range(DEC_BATCH)], axis=0)
        win = jnp.concatenate([win_e[s * lu + POOL_PAD:(s + 1) * lu] for s in range(DEC_BATCH)], axis=0)
        z_ref[...] = (gate_b * conv).astype(z_ref.dtype)
        inv = 1.0 / window.astype(jnp.float32)
        pooled_ref[...] = (win * inv - u).astype(pooled_ref.dtype)
        vs[...] = v
        us[...] = u
        for s in range(DEC_BATCH):
            end = (s + 1) * DEC_SEQ
            css_ref[s] = vs[end - CONV_HIST:end, :]
            pss_ref[s] = us[end - POOL_HIST:end, :]


def _proj(xn, w_in_bf, conv_k, cache_conv_pad, cache_pool_pad):
    nsec = D_CONV // TC
    pidx = lambda c, i: (jnp.minimum(i, N_PROMPT_TILES - 1) // 2, 0, c)
    return pl.pallas_call(
        _proj_kernel,
        grid=(nsec, N_ROW_TILES),
        in_specs=[
            pl.BlockSpec((ROW_TILE, D_MODEL), lambda c, i: (i, 0)),
            pl.BlockSpec((D_MODEL, TC), lambda c, i: (0, c)),
            pl.BlockSpec((D_MODEL, TC), lambda c, i: (0, nsec + c)),
            pl.BlockSpec((D_MODEL, TC), lambda c, i: (0, 2 * nsec + c)),
            pl.BlockSpec((D_MODEL, TC), lambda c, i: (0, 3 * nsec + c)),
            pl.BlockSpec((3, TC), lambda c, i: (0, c)),
            pl.BlockSpec((DEC_BATCH, CONV_PAD, TC), lambda c, i: (0, 0, c)),
            pl.BlockSpec((DEC_BATCH, POOL_PAD, TC), lambda c, i: (0, 0, c)),
        ],
        out_specs=[
            pl.BlockSpec((ROW_TILE, TC), lambda c, i: (i, c)),
            pl.BlockSpec((ROW_TILE, TC), lambda c, i: (i, c)),
            pl.BlockSpec((1, CONV_HIST, TC), pidx),
            pl.BlockSpec((1, POOL_HIST, TC), pidx),
            pl.BlockSpec((DEC_BATCH, CONV_HIST, TC), lambda c, i: (0, 0, c)),
            pl.BlockSpec((DEC_BATCH, POOL_HIST, TC), lambda c, i: (0, 0, c)),
        ],
        out_shape=[
            jax.ShapeDtypeStruct((T, D_CONV), jnp.bfloat16),
            jax.ShapeDtypeStruct((T, D_POOL), jnp.bfloat16),
            jax.ShapeDtypeStruct((BATCH, CONV_HIST, D_CONV), jnp.float32),
            jax.ShapeDtypeStruct((BATCH, POOL_HIST, D_POOL), jnp.float32),
            jax.ShapeDtypeStruct((DEC_BATCH, CONV_HIST, D_CONV), jnp.float32),
            jax.ShapeDtypeStruct((DEC_BATCH, POOL_HIST, D_POOL), jnp.float32),
        ],
        scratch_shapes=[
            pltpu.VMEM((CONV_PAD, TC), jnp.float32),
            pltpu.VMEM((POOL_PAD, TC), jnp.float32),
            pltpu.VMEM((ROW_TILE, TC), jnp.float32),
            pltpu.VMEM((ROW_TILE, TC), jnp.float32),
        ],
        compiler_params=_cparams(("arbitrary", "arbitrary")),
        name="mix_proj",
    )(xn, w_in_bf, w_in_bf, w_in_bf, w_in_bf, conv_k, cache_conv_pad, cache_pool_pad)


def _gates_kernel(xn_ref, w_ref, o_ref):
    g = jnp.dot(xn_ref[...], w_ref[...], preferred_element_type=jnp.float32)
    o_ref[...] = jax.nn.sigmoid(g).astype(o_ref.dtype)


def _gates(xn, w_in_bf):
    tn = 1024
    off = (3 * D_CONV + D_POOL) // tn
    return pl.pallas_call(
        _gates_kernel,
        grid=(2 * D_MODEL // tn, N_ROW_TILES),
        in_specs=[
            pl.BlockSpec((ROW_TILE, D_MODEL), lambda j, i: (i, 0)),
            pl.BlockSpec((D_MODEL, tn), lambda j, i: (0, off + j)),
        ],
        out_specs=pl.BlockSpec((ROW_TILE, tn), lambda j, i: (i, j)),
        out_shape=jax.ShapeDtypeStruct((T, 2 * D_MODEL), jnp.bfloat16),
        compiler_params=_cparams(("arbitrary", "arbitrary")),
        name="mix_gates",
    )(xn, w_in_bf)


def _merge_kernel(z_ref, wco_ref, p_ref, wpm_ref, scale_ref, ga_ref, gb_ref, o_ref):
    y_conv = jnp.dot(z_ref[...], wco_ref[...], preferred_element_type=jnp.float32)
    y_pool = jnp.dot(p_ref[...], wpm_ref[0], preferred_element_type=jnp.float32) * scale_ref[...]
    mix = ga_ref[...].astype(jnp.float32) * y_conv + gb_ref[...].astype(jnp.float32) * y_pool
    o_ref[...] = mix.astype(o_ref.dtype)


def _merge(z, wco_bf, pooled, wpm_bf, pool_scale, sg):
    tn = D_MODEL // 4
    return pl.pallas_call(
        _merge_kernel,
        grid=(4, N_ROW_TILES),
        in_specs=[
            pl.BlockSpec((ROW_TILE, D_CONV), lambda j, i: (i, 0)),
            pl.BlockSpec((D_CONV, tn), lambda j, i: (0, j)),
            pl.BlockSpec((ROW_TILE, POOL_GROUP_DIM), lambda j, i: (i, j)),
            pl.BlockSpec((1, POOL_GROUP_DIM, tn), lambda j, i: (j, 0, 0)),
            pl.BlockSpec((1, tn), lambda j, i: (0, j)),
            pl.BlockSpec((ROW_TILE, tn), lambda j, i: (i, j)),
            pl.BlockSpec((ROW_TILE, tn), lambda j, i: (i, 4 + j)),
        ],
        out_specs=pl.BlockSpec((ROW_TILE, tn), lambda j, i: (i, j)),
        out_shape=jax.ShapeDtypeStruct((T, D_MODEL), jnp.bfloat16),
        compiler_params=_cparams(("arbitrary", "arbitrary")),
        name="mix_merge",
    )(z, wco_bf, pooled, wpm_bf, pool_scale, sg, sg)


def _oproj_kernel(m_ref, w_ref, xp_ref, xs_ref, o_ref):
    i = pl.program_id(1)
    a = jnp.dot(m_ref[...], w_ref[...], preferred_element_type=jnp.float32)

    @pl.when(i < T_P // 512)
    def _():
        o_ref[...] = xp_ref[...] + a

    @pl.when(i >= T_P // 512)
    def _():
        o_ref[...] = xs_ref[...] + a


def _oproj(mix, w_o_bf, xp, xs):
    tn = 1024
    np_ = T_P // 512
    return pl.pallas_call(
        _oproj_kernel,
        grid=(D_MODEL // tn, T // 512),
        in_specs=[
            pl.BlockSpec((512, D_MODEL), lambda j, i: (i, 0)),
            pl.BlockSpec((D_MODEL, tn), lambda j, i: (0, j)),
            pl.BlockSpec((512, tn), lambda j, i: (jnp.minimum(i, np_ - 1), j)),
            pl.BlockSpec((512, tn), lambda j, i: (jnp.maximum(i - np_, 0), j)),
        ],
        out_specs=pl.BlockSpec((512, tn), lambda j, i: (i, j)),
        out_shape=jax.ShapeDtypeStruct((T, D_MODEL), jnp.float32),
        compiler_params=_cparams(("arbitrary", "arbitrary")),
        name="mix_oproj",
    )(mix, w_o_bf, xp, xs)


def _pack_bf16_pair(lo, hi):
    lo_bits = pltpu.bitcast(lo.astype(jnp.bfloat16).astype(jnp.float32), jnp.uint32)
    hi_bits = pltpu.bitcast(hi.astype(jnp.bfloat16).astype(jnp.float32), jnp.uint32)
    return jnp.right_shift(lo_bits, jnp.uint32(16)) | (hi_bits & jnp.uint32(0xFFFF0000))


def _unpack_f32_pair(w):
    lo = pltpu.bitcast(jnp.left_shift(w, jnp.uint32(16)), jnp.float32)
    hi = pltpu.bitcast(w & jnp.uint32(0xFFFF0000), jnp.float32)
    return lo, hi


def _route_kernel(h_ref, g_ref, wr_ref, br_ref, hn_ref, idx_ref, gate_ref, rank_ref, hist_ref):
    h = h_ref[...]
    hn = h * lax.rsqrt(jnp.mean(h * h, axis=-1, keepdims=True) + EPS) * g_ref[...]
    hn_ref[...] = _pack_bf16_pair(hn[:, :HALF], hn[:, HALF:])
    def split(x):
        hi = x.astype(jnp.bfloat16)
        return hi, (x - hi.astype(jnp.float32)).astype(jnp.bfloat16)

    def dot_nt(a, b):
        return lax.dot_general(a, b, (((1,), (1,)), ((), ())), preferred_element_type=jnp.float32)

    w_hi, w_lo = split(wr_ref[...])
    hn_hi, hn_lo = split(hn)
    logits = dot_nt(w_hi, hn_hi) + dot_nt(w_hi, hn_lo) + dot_nt(w_lo, hn_hi) + br_ref[...]
    eid = lax.broadcasted_iota(jnp.int32, logits.shape, 0)
    vals, ids = [], []
    for _ in range(TOP_K):
        m = jnp.max(logits, axis=0, keepdims=True)
        sel = jnp.min(jnp.where(logits == m, eid, N_EXPERTS), axis=0, keepdims=True)
        vals.append(m)
        ids.append(sel)
        logits = jnp.where(eid == sel, -jnp.inf, logits)
    ex = [jnp.exp(v - vals[0]) for v in vals]
    den = ex[0] + ex[1] + ex[2] + ex[3]
    for k in range(TOP_K):
        idx_ref[k:k + 1, :] = ids[k]
        gate_ref[k:k + 1, :] = ex[k] / den
    tm = logits.shape[1]
    tri = (lax.broadcasted_iota(jnp.int32, (tm, tm), 0)
           <= lax.broadcasted_iota(jnp.int32, (tm, tm), 1)).astype(jnp.bfloat16)
    run = jnp.zeros((N_EXPERTS, 1), jnp.float32)
    for k in range(TOP_K):
        oh = (eid == ids[k]).astype(jnp.float32)
        seen = jnp.dot(oh.astype(jnp.bfloat16), tri, preferred_element_type=jnp.float32)
        rank = jnp.sum(oh * (seen - 1.0 + run), axis=0, keepdims=True)
        rank_ref[k:k + 1, :] = rank.astype(jnp.int32)
        run = run + jnp.sum(oh, axis=1, keepdims=True)
    hist_ref[0] = jnp.broadcast_to(run, (N_EXPERTS, 128)).astype(jnp.int32)


def _route(h, g, wr_t, br):
    tm = ROUTE_TILE
    return pl.pallas_call(
        _route_kernel,
        grid=(T // tm,),
        in_specs=[
            pl.BlockSpec((tm, D_MODEL), lambda i: (i, 0)),
            pl.BlockSpec((1, D_MODEL), lambda i: (0, 0)),
            pl.BlockSpec((N_EXPERTS, D_MODEL), lambda i: (0, 0)),
            pl.BlockSpec((N_EXPERTS, 1), lambda i: (0, 0)),
        ],
        out_specs=[
            pl.BlockSpec((tm, HALF), lambda i: (i, 0)),
            pl.BlockSpec((TOP_K, tm), lambda i: (0, i)),
            pl.BlockSpec((TOP_K, tm), lambda i: (0, i)),
            pl.BlockSpec((TOP_K, tm), lambda i: (0, i)),
            pl.BlockSpec((1, N_EXPERTS, 128), lambda i: (i, 0, 0)),
        ],
        out_shape=[
            jax.ShapeDtypeStruct((T, HALF), jnp.uint32),
            jax.ShapeDtypeStruct((TOP_K, T), jnp.int32),
            jax.ShapeDtypeStruct((TOP_K, T), jnp.float32),
            jax.ShapeDtypeStruct((TOP_K, T), jnp.int32),
            jax.ShapeDtypeStruct((T // tm, N_EXPERTS, 128), jnp.int32),
        ],
        compiler_params=_cparams(("arbitrary",)),
        name="ffn_route",
    )(h, g, wr_t, br)


def _unpack_rows(xw):
    lo, hi = _unpack_f32_pair(xw)
    return lo.astype(jnp.bfloat16), hi.astype(jnp.bfloat16)


def _experts_kernel(ue_ref, ub_ref, un_ref, meta_ref, tok_ref,
                    bgu_ref, bd_ref, hn_hbm, wgu_hbm, wd_hbm, y_hbm,
                    xbuf, act, wstage, wbf, ystage, gsem, wsem, ysem):
    u = pl.program_id(0)
    n_units = meta_ref[0]
    n_used_blocks = meta_ref[1]
    dot = functools.partial(jnp.dot, preferred_element_type=jnp.float32)

    def gather_copy(unit, r):
        tok = tok_ref[ub_ref[unit] * EB + r]
        return pltpu.make_async_copy(hn_hbm.at[pl.ds(tok, 1), :], xbuf.at[pl.ds(r, 1), :], gsem)

    def start_gather(unit):
        def body(q, carry):
            for d in range(GATHER_UNROLL):
                gather_copy(unit, q * GATHER_UNROLL + d).start()
            return carry
        lax.fori_loop(0, UNIT_ROWS // GATHER_UNROLL, body, 0)

    def start_gather_part(unit, part):
        for d in range(GATHER_PER_TILE):
            gather_copy(unit, part * GATHER_PER_TILE + d).start()

    def wait_gather():
        pltpu.make_async_copy(hn_hbm.at[pl.ds(0, UNIT_ROWS), :], xbuf, gsem).wait()

    def tile_copies(w_hbm, e, cols, slot):
        out = []
        chunk = D_MODEL // W_ROW_SPLIT
        for part, c0 in enumerate(cols):
            for kh in range(W_ROW_SPLIT):
                rows = pl.ds(kh * chunk, chunk)
                out.append(pltpu.make_async_copy(w_hbm.at[e, rows, pl.ds(c0, TW)],
                                                 wstage.at[slot, part, rows, :], wsem.at[slot]))
        return out

    def gate_up_copies(e, j, slot):
        c0 = pl.multiple_of(j * TW, TW)
        return tile_copies(wgu_hbm, e, (c0, D_FF + c0), slot)

    def down_copies(e, j, slot):
        c0 = pl.multiple_of(j * 2 * TW, 2 * TW)
        return tile_copies(wd_hbm, e, (c0, c0 + TW), slot)

    def start(copies):
        for c in copies:
            c.start()

    def wait_tile(slot):
        pltpu.make_async_copy(wgu_hbm.at[pl.ds(0, 2), :, pl.ds(0, TW)], wstage.at[slot], wsem.at[slot]).wait()

    def cast_tile(slot):
        wbf[:, :TW] = wstage[slot, 0].astype(jnp.bfloat16)
        wbf[:, TW:] = wstage[slot, 1].astype(jnp.bfloat16)

    def y_copy(block, j, ys):
        r0 = pl.multiple_of(block * EB, EB)
        c0 = pl.multiple_of(j * TW, TW)
        return pltpu.make_async_copy(ystage.at[ys], y_hbm.at[pl.ds(r0, UNIT_ROWS), pl.ds(c0, TW)],
                                     ysem.at[ys])

    def zero_copy(block, j):
        r0 = pl.multiple_of(block * EB, EB)
        c0 = pl.multiple_of(j * TW, TW)
        return pltpu.make_async_copy(ystage.at[0, pl.ds(0, EB), :],
                                     y_hbm.at[pl.ds(r0, EB), pl.ds(c0, TW)], ysem.at[0])

    def gate_up_rows(r0, j, bg, bl, m=EB):
        lo, hi = _unpack_rows(xbuf[pl.ds(r0, m), :])
        hcat = dot(jnp.concatenate([lo, hi], axis=1), wbf[...])
        glu = jnp.minimum(hcat[:, :TW] + bg, SWIGLU_LIMIT)
        lin = jnp.clip(hcat[:, TW:] + bl, -SWIGLU_LIMIT, SWIGLU_LIMIT)
        a = glu * jax.nn.sigmoid(SWIGLU_ALPHA * glu) * (lin + 1.0)
        act[j, pl.ds(r0, m), :] = a.astype(act.dtype)

    def down_rows(r0, ys, bd, m=EB):
        a = jnp.concatenate([act[jj, pl.ds(r0, m), :] for jj in range(NT1)], axis=1)
        y = dot(a, wbf[...]) + bd
        ystage[ys, pl.ds(r0, m), :] = _pack_bf16_pair(y[:, :TW], y[:, TW:])

    def for_blocks_after_first(nb, block_fn):
        rest = nb - 1
        for run in (4, 2, 1):
            @pl.when((rest & run) != 0)
            def _(run=run):
                first = 1 + (rest & ~(2 * run - 1))
                for q in range(run):
                    block_fn(pl.multiple_of((first + q) * EB, EB), EB)

    @pl.when(u == 0)
    def _prologue():
        ystage[...] = jnp.zeros(ystage.shape, ystage.dtype)
        start_gather(0)
        start(gate_up_copies(ue_ref[0], 0, 0))

    @pl.when(u < n_units)
    def _unit():
        e = ue_ref[u]
        b0 = ub_ref[u]
        nb = un_ref[u]
        has_next = u + 1 < n_units
        wait_gather()

        def gate_up_tile(j, carry):
            slot = j % 2
            wait_tile(slot)

            @pl.when(j + 1 < NT1)
            def _():
                start(gate_up_copies(e, j + 1, 1 - slot))

            @pl.when(j + 1 == NT1)
            def _():
                start(down_copies(e, 0, 1 - slot))

            bg = bgu_ref[0, pl.ds(j, 1), :]
            bl = bgu_ref[0, pl.ds(NT1 + j, 1), :]
            cast_tile(slot)
            gate_up_rows(0, j, bg, bl)
            for_blocks_after_first(nb, lambda r0, m: gate_up_rows(r0, j, bg, bl, m))
            return carry

        lax.fori_loop(0, NT1, gate_up_tile, 0)

        def down_tile(j, carry):
            slot = j % 2
            wait_tile(slot)

            @pl.when(j + 1 < NT2)
            def _():
                start(down_copies(e, j + 1, 1 - slot))

            @pl.when((j + 1 == NT2) & has_next)
            def _():
                start(gate_up_copies(ue_ref[u + 1], 0, 1 - slot))

            bd = bd_ref[0, pl.ds(j, 1), :]
            ys = j % 2

            @pl.when(j >= 2)
            def _():
                y_copy(b0, j, ys).wait()

            cast_tile(slot)
            start_gather_part(u + 1, j)
            down_rows(0, ys, bd)
            for_blocks_after_first(nb, lambda r0, m: down_rows(r0, ys, bd, m))
            y_copy(b0, j, ys).start()
            return carry

        lax.fori_loop(0, NT2, down_tile, 0)
        y_copy(b0, 0, 0).wait()
        y_copy(b0, 0, 1).wait()

        @pl.when(jnp.logical_not(has_next))
        def _():
            wait_gather()

    @pl.when(u == N_UNITS - 1)
    def _zero_unused_rows():
        ystage[0] = jnp.zeros(ystage.shape[1:], ystage.dtype)

        def body(b, carry):
            for j in range(NT2):
                zero_copy(b, j).start()
            for j in range(NT2):
                zero_copy(b, j).wait()
            return carry

        lax.fori_loop(n_used_blocks, N_BLOCKS + UNIT_BLOCKS, body, 0)


def _experts(unit_tables, row_tok, hn_packed, w_gate_up, b_gate_up, w_down, b_down):
    any_spec = pl.BlockSpec(memory_space=pl.ANY)
    return pl.pallas_call(
        _experts_kernel,
        grid_spec=pltpu.PrefetchScalarGridSpec(
            num_scalar_prefetch=5,
            grid=(N_UNITS,),
            in_specs=[
                pl.BlockSpec((1, 2 * NT1, TW), lambda u, ue, ub, un, meta, tok: (ue[u], 0, 0)),
                pl.BlockSpec((1, NT2, 2 * TW), lambda u, ue, ub, un, meta, tok: (ue[u], 0, 0)),
                any_spec, any_spec, any_spec,
            ],
            out_specs=any_spec,
            scratch_shapes=[
                pltpu.VMEM((UNIT_ROWS, HALF), jnp.uint32),
                pltpu.VMEM((NT1, UNIT_ROWS, TW), jnp.bfloat16),
                pltpu.VMEM((2, 2, D_MODEL, TW), jnp.float32),
                pltpu.VMEM((D_MODEL, 2 * TW), jnp.bfloat16),
                pltpu.VMEM((2, UNIT_ROWS, TW), jnp.uint32),
                pltpu.SemaphoreType.DMA(()),
                pltpu.SemaphoreType.DMA((2,)),
                pltpu.SemaphoreType.DMA((2,)),
            ],
        ),
        out_shape=jax.ShapeDtypeStruct((N_ROWS + UNIT_ROWS, HALF), jnp.uint32),
        compiler_params=_cparams(("arbitrary",)),
        name="ffn_experts",
    )(*unit_tables, row_tok, b_gate_up, b_down, hn_packed, w_gate_up, w_down)


def _combine_kernel(pos_ref, h_ref, gate_ref, g_ref, y_hbm, o_ref, rows, sem, *, tok0):
    i = pl.program_id(0)
    slot = i % 2

    def start_rows(step, dst_slot):
        base = tok0 + step * COMB_TOK

        def body(t, carry):
            for k in range(TOP_K):
                p = pos_ref[k * T + base + t]
                pltpu.make_async_copy(y_hbm.at[pl.ds(p, 1), :], rows.at[dst_slot, k, pl.ds(t, 1), :],
                                      sem.at[dst_slot]).start()
            return carry

        lax.fori_loop(0, COMB_TOK, body, 0)

    @pl.when(i == 0)
    def _():
        start_rows(0, 0)

    @pl.when(i + 1 < pl.num_programs(0))
    def _():
        start_rows(i + 1, 1 - slot)

    for k in range(TOP_K):
        pltpu.make_async_copy(y_hbm.at[pl.ds(0, COMB_TOK), :], rows.at[slot, k], sem.at[slot]).wait()
    gate = gate_ref[...]
    acc_lo = acc_hi = None
    for k in range(TOP_K):
        lo, hi = _unpack_f32_pair(rows[slot, k])
        g = gate[:, k:k + 1]
        acc_lo = lo * g if k == 0 else acc_lo + lo * g
        acc_hi = hi * g if k == 0 else acc_hi + hi * g
    pieces = []
    for j in range(NT2):
        pieces += [acc_lo[:, j * TW:(j + 1) * TW], acc_hi[:, j * TW:(j + 1) * TW]]
    x = h_ref[...] + jnp.concatenate(pieces, axis=1)
    y = x * lax.rsqrt(jnp.mean(x * x, axis=-1, keepdims=True) + EPS)
    o_ref[...] = y * g_ref[...]


def _combine(pos_flat, h, gate_tk, g_final, y_sorted, tok0, n_tok):
    blk0 = tok0 // COMB_TOK
    return pl.pallas_call(
        functools.partial(_combine_kernel, tok0=tok0),
        grid_spec=pltpu.PrefetchScalarGridSpec(
            num_scalar_prefetch=1,
            grid=(n_tok // COMB_TOK,),
            in_specs=[
                pl.BlockSpec((COMB_TOK, D_MODEL), lambda i, pos: (blk0 + i, 0)),
                pl.BlockSpec((COMB_TOK, TOP_K), lambda i, pos: (blk0 + i, 0)),
                pl.BlockSpec((1, D_MODEL), lambda i, pos: (0, 0)),
                pl.BlockSpec(memory_space=pl.ANY),
            ],
            out_specs=pl.BlockSpec((COMB_TOK, D_MODEL), lambda i, pos: (i, 0)),
            scratch_shapes=[
                pltpu.VMEM((2, TOP_K, COMB_TOK, HALF), jnp.uint32),
                pltpu.SemaphoreType.DMA((2,)),
            ],
        ),
        out_shape=jax.ShapeDtypeStruct((n_tok, D_MODEL), jnp.float32),
        compiler_params=_cparams(("arbitrary",)),
        name="ffn_combine",
    )(pos_flat, h, gate_tk, g_final, y_sorted)


def _routing(idx_t, rank_t, hist):
    i32 = jnp.int32
    tile_off = jnp.cumsum(hist, axis=0) - hist
    counts = jnp.sum(hist, axis=0)
    nblk = (counts + EB - 1) // EB
    blk_end = jnp.cumsum(nblk)
    blk_start = blk_end - nblk
    base_tab = tile_off + blk_start[None, :] * EB
    n_tiles = T // ROUTE_TILE
    onehot = idx_t.reshape(TOP_K, n_tiles, ROUTE_TILE, 1) == jnp.arange(N_EXPERTS, dtype=i32)
    base = jnp.sum(jnp.where(onehot, base_tab[None, :, None, :], 0), axis=-1)
    dest = (base.reshape(TOP_K, T) + rank_t).reshape(-1).astype(i32)
    tok = jnp.tile(jnp.arange(T, dtype=i32), TOP_K)
    row_tok = jnp.zeros((N_ROWS + UNIT_ROWS,), i32).at[dest].set(tok, unique_indices=True)

    n_unit_e = (nblk + UNIT_BLOCKS - 1) // UNIT_BLOCKS
    unit_end = jnp.cumsum(n_unit_e)
    n_units = unit_end[-1]
    u = jnp.minimum(jnp.arange(N_UNITS, dtype=i32), n_units - 1)
    e = jnp.minimum(jnp.sum(unit_end[None, :] <= u[:, None], axis=1), N_EXPERTS - 1).astype(i32)
    local = u - (unit_end[e] - n_unit_e[e])
    b0 = blk_start[e] + local * UNIT_BLOCKS
    nb = jnp.minimum(nblk[e] - local * UNIT_BLOCKS, UNIT_BLOCKS)
    meta = jnp.stack([n_units, blk_end[-1]]).astype(i32)
    return row_tok, dest, (e, b0.astype(i32), nb.astype(i32), meta)


def kernel(x_prompt, x_sample, cache_conv, cache_pool, norm_mix_g, w_in, conv_k, w_conv_out, w_pool_map,
           pool_scale, w_o, norm_ffn_g, w_router, b_router, w_gate_up, b_gate_up, w_down, b_down,
           norm_final_g):
    bf = jnp.bfloat16
    xp = x_prompt.reshape(T_P, D_MODEL)
    xs = x_sample.reshape(T_S, D_MODEL)
    cc_pad = jnp.pad(cache_conv[0], ((0, 0), (CONV_PAD - CONV_HIST, 0), (0, 0)))
    cp_pad = jnp.pad(cache_pool[0], ((0, 0), (POOL_PAD - POOL_HIST, 0), (0, 0)))

    xn = _norm(xp, xs, norm_mix_g)
    w_in_bf = w_in[0].astype(bf)
    z, pooled, cs_p, ps_p, cs_s, ps_s = _proj(xn, w_in_bf, conv_k[0], cc_pad, cp_pad)
    sg = _gates(xn, w_in_bf)
    mix = _merge(z, w_conv_out[0].astype(bf), pooled, w_pool_map[0].astype(bf), pool_scale, sg)
    h = _oproj(mix, w_o[0].astype(bf), xp, xs)

    hn_packed, idx_t, gate_t, rank_t, hist = _route(h, norm_ffn_g, w_router[0].T,
                                                    b_router[0].reshape(N_EXPERTS, 1))
    row_tok, dest, unit_tables = _routing(idx_t, rank_t, hist[:, :, 0])
    y_sorted = _experts(unit_tables, row_tok, hn_packed,
                        w_gate_up[0], b_gate_up[0].reshape(N_EXPERTS, 2 * NT1, TW),
                        w_down[0], b_down[0].reshape(N_EXPERTS, NT2, 2 * TW))

    gate_tk = gate_t.T
    g_final = norm_final_g.reshape(1, D_MODEL)
    y_p = _combine(dest, h, gate_tk, g_final, y_sorted, 0, T_P)
    y_s = _combine(dest, h, gate_tk, g_final, y_sorted, T_P, T_S)
    return (y_p.reshape(BATCH, SEQ, D_MODEL), y_s.reshape(DEC_BATCH, DEC_SEQ, D_MODEL),
            cs_p[None], ps_p[None], cs_s[None], ps_s[None])
```
